```python
import math
import jax
import jax.numpy as jnp
from jax import lax
import numpy as np

D_MODEL = 1024
BATCH = 1
SEQ = 16384
DEPTH = 2

N_META = 16
LEAD = 128
N_PAD = LEAD - N_META
M_HEADS = 4
M_HEAD_DIM = D_MODEL // M_HEADS
M_INNER = M_HEADS * M_HEAD_DIM
CONV_W = 4
CHUNK = 64
F_BIAS_LO = 3.0
F_BIAS_HI = 6.0
A_HEADS = 8
A_HEAD_DIM = D_MODEL // (2 * A_HEADS)
A_VDIM = 2 * A_HEAD_DIM
A_INNER = A_HEADS * A_VDIM
Q_BLOCK = 128
N_EXPERTS = 16
N_GROUPS = 4
EXPERTS_PER_GROUP = N_EXPERTS // N_GROUPS
TOP_K = 2
D_EXPERT = D_MODEL
DN_ALPHA = (2 * DEPTH) ** 0.25
DN_BETA = (8 * DEPTH) ** -0.25
LN_EPS = 1e-5
NEG = -1e30
SPLITS = (M_INNER, M_INNER, M_HEADS, M_HEADS, A_INNER, A_INNER, A_INNER, D_MODEL, D_MODEL)
D_IN = 7176

kernel_name = 'hybrid_mlstm_diffattn_groupmoe_block'


def layer_norm(x, g, b):
    xf = x.astype(jnp.float32)
    mu = jnp.mean(xf, axis=-1, keepdims=True)
    var = jnp.mean(jnp.square(xf - mu), axis=-1, keepdims=True)
    y = (xf - mu) * lax.rsqrt(var + LN_EPS) * g.astype(jnp.float32) + b.astype(jnp.float32)
    return y.astype(x.dtype)


def causal_conv(x, w, b):
    y = lax.conv_general_dilated(
        x, w[:, None, :].astype(x.dtype), window_strides=(1,),
        padding=[(CONV_W - 1, 0)], dimension_numbers=('NWC', 'WIO', 'NWC'),
        feature_group_count=x.shape[-1])
    return y + b.astype(x.dtype)


def to_chunks(t):
    bsz, length = t.shape[:2]
    t = t.reshape((bsz, length // CHUNK, CHUNK) + t.shape[2:])
    return jnp.swapaxes(jnp.moveaxis(t, 1, 0), 2, 3)


def mlstm_chunk_step(carry, inp):
    c_state, n_state, m_state = carry
    q, k, v, logi, logf = inp
    causal = jnp.tril(jnp.ones((CHUNK, CHUNK), dtype=bool))
    b = jnp.cumsum(logf, axis=-1)
    g = b[..., -1]
    d_mat = jnp.where(causal, b[..., :, None] - b[..., None, :] + logi[..., None, :], NEG)
    inter = b + m_state[..., None]
    m_t = jnp.maximum(inter, jnp.max(d_mat, axis=-1))
    w_intra = jnp.einsum('bhtd,bhsd->bhts', q, k) * jnp.exp(d_mat - m_t[..., None])
    scale_inter = jnp.exp(inter - m_t)
    num = (jnp.einsum('bhts,bhsv->bhtv', w_intra, v)
           + scale_inter[..., None] * jnp.einsum('bhtd,bhdv->bhtv', q, c_state))
    den = jnp.sum(w_intra, axis=-1) + scale_inter * jnp.einsum('bhtd,bhd->bht', q, n_state)
    h = num / jnp.maximum(jnp.abs(den), jnp.exp(-m_t))[..., None]
    a = g[..., None] - b + logi
    m_new = jnp.maximum(g + m_state, jnp.max(a, axis=-1))
    w_state = jnp.exp(a - m_new[..., None])
    decay = jnp.exp(g + m_state - m_new)
    c_new = decay[..., None, None] * c_state + jnp.einsum('bhs,bhsd,bhsv->bhdv', w_state, k, v)
    n_new = decay[..., None] * n_state + jnp.einsum('bhs,bhsd->bhd', w_state, k)
    return (c_new, n_new, m_new), h


def mlstm_branch(xm, o_pre, i_pre, f_pre, valid, conv_w, conv_b, wq, wk, wv, b_i, b_f, norm_g):
    bsz, length, _ = xm.shape
    f32 = jnp.float32
    xm = xm * valid[None, :, None].astype(xm.dtype)
    xc = jax.nn.silu(causal_conv(xm, conv_w, conv_b))
    xch = xc.reshape(bsz, length, M_HEADS, M_HEAD_DIM)
    xmh = xm.reshape(bsz, length, M_HEADS, M_HEAD_DIM)
    q = jnp.einsum('blhd,hde->blhe', xch, wq).astype(f32)
    k = (jnp.einsum('blhd,hde->blhe', xch, wk) * (M_HEAD_DIM ** -0.5)).astype(f32)
    v = jnp.einsum('blhd,hde->blhe', xmh, wv).astype(f32)
    vmask = valid[None, :, None]
    logi = jnp.where(vmask, i_pre.astype(f32) + b_i.astype(f32), NEG)
    logf = jnp.where(vmask, jax.nn.log_sigmoid(f_pre.astype(f32) + b_f.astype(f32)), 0.0)
    init = (jnp.zeros((bsz, M_HEADS, M_HEAD_DIM, M_HEAD_DIM), f32),
            jnp.zeros((bsz, M_HEADS, M_HEAD_DIM), f32),
            jnp.zeros((bsz, M_HEADS), f32))
    xs = (to_chunks(q), to_chunks(k), to_chunks(v), to_chunks(logi), to_chunks(logf))
    _, h = lax.scan(mlstm_chunk_step, init, xs)
    h = jnp.moveaxis(jnp.swapaxes(h, 2, 3), 0, 1).reshape(bsz, length, M_HEADS, M_HEAD_DIM)
    h = jax.nn.sigmoid(o_pre.astype(f32)).reshape(bsz, length, M_HEADS, M_HEAD_DIM) * h
    mu = jnp.mean(h, axis=-1, keepdims=True)
    var = jnp.mean(jnp.square(h - mu), axis=-1, keepdims=True)
    h = ((h - mu) * lax.rsqrt(var + LN_EPS)).reshape(bsz, length, M_INNER) * norm_g.astype(f32)
    return h.astype(xm.dtype)


def diff_attn_branch(q, k, v, valid, lq1, lk1, lq2, lk2, norm_g, lam_init):
    bsz, length, _ = q.shape
    f32 = jnp.float32
    q = q.reshape(bsz, length, A_HEADS, 2, A_HEAD_DIM) * (A_HEAD_DIM ** -0.5)
    k = k.reshape(bsz, length, A_HEADS, 2, A_HEAD_DIM)
    v = v.reshape(bsz, length, A_HEADS, A_VDIM)
    k1 = k[..., 0, :]
    k2 = k[..., 1, :]
    lam = (jnp.exp(jnp.sum(lq1.astype(f32) * lk1.astype(f32)))
           - jnp.exp(jnp.sum(lq2.astype(f32) * lk2.astype(f32))) + lam_init)
    n_blocks = length // Q_BLOCK
    qb = jnp.moveaxis(q.reshape(bsz, n_blocks, Q_BLOCK, A_HEADS, 2, A_HEAD_DIM), 1, 0)
    starts = jnp.arange(n_blocks) * Q_BLOCK
    kpos = jnp.arange(length)

    def block(args):
        qblk, start = args
        qpos = start + jnp.arange(Q_BLOCK)
        mask = (kpos[None, :] <= qpos[:, None]) & valid[None, :]
        s1 = jnp.einsum('bqhd,bkhd->bhqk', qblk[..., 0, :], k1).astype(f32)
        s2 = jnp.einsum('bqhd,bkhd->bhqk', qblk[..., 1, :], k2).astype(f32)
        a1 = jax.nn.softmax(jnp.where(mask, s1, NEG), axis=-1)
        a2 = jax.nn.softmax(jnp.where(mask, s2, NEG), axis=-1)
        attn = a1 - lam * a2
        return jnp.einsum('bhqk,bkhv->bqhv', attn.astype(v.dtype), v)

    o = lax.map(block, (qb, starts))
    o = jnp.moveaxis(o, 0, 1).reshape(bsz, length, A_HEADS, A_VDIM).astype(f32)
    o = o * lax.rsqrt(jnp.mean(jnp.square(o), axis=-1, keepdims=True) + LN_EPS)
    o = o * norm_g.astype(f32) * (1.0 - lam_init)
    return o.reshape(bsz, length, A_INNER).astype(v.dtype)


def group_moe(x, w_router, router_bias, w_gate, w_up, w_down):
    bsz, length, dm = x.shape
    f32 = jnp.float32
    xt = x.reshape(-1, dm)
    probs = jax.nn.softmax((xt @ w_router).astype(f32), axis=-1)
    sel = probs + router_bias.astype(f32)
    group_score = jnp.max(sel.reshape(-1, N_GROUPS, EXPERTS_PER_GROUP), axis=-1)
    top_group = jnp.argmax(group_score, axis=-1)
    in_group = (jnp.arange(N_EXPERTS) // EXPERTS_PER_GROUP)[None, :] == top_group[:, None]
    _, idx = lax.top_k(jnp.where(in_group, sel, NEG), TOP_K)
    w = jnp.take_along_axis(probs, idx, axis=-1)
    w = w / jnp.sum(w, axis=-1, keepdims=True)
    combine = jnp.einsum('tk,tke->te', w, jax.nn.one_hot(idx, N_EXPERTS, dtype=f32))
    y = jnp.zeros((xt.shape[0], dm), f32)
    for e in range(N_EXPERTS):
        h = jax.nn.silu(xt @ w_gate[e]) * (xt @ w_up[e])
        y = y + combine[:, e:e + 1] * (h @ w_down[e]).astype(f32)
    return y.astype(x.dtype).reshape(bsz, length, dm)


def setup_inputs(seed: int = 0) -> dict:
    key = jax.random.key(seed)
    ks = jax.random.split(key, 32)
    f32 = jnp.float32

    def nrm(k, shape, scale):
        return jax.random.normal(k, shape, f32) * scale

    return {
        'x': nrm(ks[0], (BATCH, SEQ, D_MODEL), 1.0),
        'meta_tokens': nrm(ks[1], (N_META, D_MODEL), 1.0),
        'ln_in_g': 1.0 + nrm(ks[2], (D_MODEL,), 0.02),
        'ln_in_b': nrm(ks[3], (D_MODEL,), 0.02),
        'w_in': nrm(ks[4], (DEPTH, D_MODEL, D_IN), D_MODEL ** -0.5),
        'conv_w': nrm(ks[5], (DEPTH, CONV_W, M_INNER), CONV_W ** -0.5),
        'conv_b': nrm(ks[6], (DEPTH, M_INNER), 0.02),
        'wq_m': nrm(ks[7], (DEPTH, M_HEADS, M_HEAD_DIM, M_HEAD_DIM), M_HEAD_DIM ** -0.5),
        'wk_m': nrm(ks[8], (DEPTH, M_HEADS, M_HEAD_DIM, M_HEAD_DIM), M_HEAD_DIM ** -0.5),
        'wv_m': nrm(ks[9], (DEPTH, M_HEADS, M_HEAD_DIM, M_HEAD_DIM), M_HEAD_DIM ** -0.5),
        'b_i': nrm(ks[10], (DEPTH, M_HEADS), 0.1),
        'b_f': jnp.linspace(F_BIAS_LO, F_BIAS_HI, M_HEADS, dtype=f32)[None, :] + nrm(ks[11], (DEPTH, M_HEADS), 0.1),
        'm_norm_g': 1.0 + nrm(ks[12], (DEPTH, M_INNER), 0.02),
        'lam_q1': nrm(ks[13], (DEPTH, A_HEAD_DIM), 0.1),
        'lam_k1': nrm(ks[14], (DEPTH, A_HEAD_DIM), 0.1),
        'lam_q2': nrm(ks[15], (DEPTH, A_HEAD_DIM), 0.1),
        'lam_k2': nrm(ks[16], (DEPTH, A_HEAD_DIM), 0.1),
        'a_norm_g': 1.0 + nrm(ks[17], (DEPTH, A_VDIM), 0.02),
        'w_branch_m': nrm(ks[18], (DEPTH, M_INNER, D_MODEL), M_INNER ** -0.5),
        'w_branch_a': nrm(ks[19], (DEPTH, A_INNER, D_MODEL), A_INNER ** -0.5),
        'w_out': nrm(ks[20], (DEPTH, D_MODEL, D_MODEL), D_MODEL ** -0.5 * DN_BETA),
        'ln1_g': 1.0 + nrm(ks[21], (DEPTH, D_MODEL), 0.02),
        'ln1_b': nrm(ks[22], (DEPTH, D_MODEL), 0.02),
        'ln2_g': 1.0 + nrm(ks[23], (DEPTH, D_MODEL), 0.02),
        'ln2_b': nrm(ks[24], (DEPTH, D_MODEL), 0.02),
        'w_router': nrm(ks[25], (D_MODEL, N_EXPERTS), D_MODEL ** -0.5),
        'router_bias': nrm(ks[26], (N_EXPERTS,), 0.01),
        'w_gate_e': nrm(ks[27], (DEPTH, N_EXPERTS, D_MODEL, D_EXPERT), D_MODEL ** -0.5),
        'w_up_e': nrm(ks[28], (DEPTH, N_EXPERTS, D_MODEL, D_EXPERT), D_MODEL ** -0.5),
        'w_down_e': nrm(ks[29], (DEPTH, N_EXPERTS, D_EXPERT, D_MODEL), D_EXPERT ** -0.5 * DN_BETA),
    }


def reference(x, meta_tokens, ln_in_g, ln_in_b, w_in, conv_w, conv_b, wq_m, wk_m, wv_m,
              b_i, b_f, m_norm_g, lam_q1, lam_k1, lam_q2, lam_k2, a_norm_g,
              w_branch_m, w_branch_a, w_out, ln1_g, ln1_b, ln2_g, ln2_b,
              w_router, router_bias, w_gate_e, w_up_e, w_down_e):
    bsz, seq, dm = x.shape
    length = seq + LEAD
    pad = jnp.zeros((bsz, N_PAD, dm), x.dtype)
    meta = jnp.broadcast_to(meta_tokens[None].astype(x.dtype), (bsz, N_META, dm))
    h = jnp.concatenate([pad, meta, x], axis=1)
    valid = jnp.arange(length) >= N_PAD
    h = layer_norm(h, ln_in_g, ln_in_b)
    split_points = [int(s) for s in np.cumsum(SPLITS)[:-1]]
    for l in range(DEPTH):
        proj = h @ w_in[l]
        xm, o_pre, i_pre, f_pre, qa, ka, va, ga, gb = jnp.split(proj, split_points, axis=-1)
        y_m = mlstm_branch(xm, o_pre, i_pre, f_pre, valid, conv_w[l], conv_b[l],
                           wq_m[l], wk_m[l], wv_m[l], b_i[l], b_f[l], m_norm_g[l])
        lam_init = 0.8 - 0.6 * math.exp(-0.3 * l)
        y_a = diff_attn_branch(qa, ka, va, valid, lam_q1[l], lam_k1[l], lam_q2[l], lam_k2[l],
                               a_norm_g[l], lam_init)
        merged = jax.nn.sigmoid(ga) * (y_m @ w_branch_m[l]) + jax.nn.sigmoid(gb) * (y_a @ w_branch_a[l])
        h = layer_norm(DN_ALPHA * h + merged @ w_out[l], ln1_g[l], ln1_b[l])
        ffn = group_moe(h, w_router, router_bias, w_gate_e[l], w_up_e[l], w_down_e[l])
        h = layer_norm(DN_ALPHA * h + ffn, ln2_g[l], ln2_b[l])
    return h[:, LEAD:]
```

```python
import functools
import math

import jax
import jax.numpy as jnp
from jax import lax
from jax.experimental import pallas as pl
from jax.experimental.pallas import tpu as pltpu

F32 = jnp.float32
BF16 = jnp.bfloat16

D_MODEL = 1024
N_META = 16
LEAD = 128
N_PAD = LEAD - N_META
M_HEADS = 4
M_HEAD_DIM = D_MODEL // M_HEADS
CONV_W = 4
A_HEADS = 8
A_HEAD_DIM = D_MODEL // (2 * A_HEADS)
A_VDIM = 2 * A_HEAD_DIM
N_EXPERTS = 16
N_GROUPS = 4
EXPERTS_PER_GROUP = N_EXPERTS // N_GROUPS
LN_EPS = 1e-5
NEG = -1e30

LANES = 128
BF16_SUBLANES = 16
VMEM_LIMIT_BYTES = 56 * 1024 * 1024

ROW_ALIGN = 1536
LN_TILE = 512
PROJ_TILE_M = 1536
PROJ_TILE_N = 1024
M_CHUNK = 256
ATTN_TQ = 512
ATTN_TK = 512
MERGE_TILE = 512
MOE_TILE = 768

COL_XM, COL_OPRE, COL_Q, COL_K, COL_V, COL_GA, COL_GB = range(7)
N_PROJ_BLOCKS = 7


def _params(semantics):
    return pltpu.CompilerParams(dimension_semantics=semantics,
                                vmem_limit_bytes=VMEM_LIMIT_BYTES)


def _layer_norm(x, g, b):
    mu = jnp.mean(x, axis=-1, keepdims=True)
    xc = x - mu
    var = jnp.mean(xc * xc, axis=-1, keepdims=True)
    return xc * lax.rsqrt(var + LN_EPS) * g + b


def _sigmoid(x):
    return 1.0 / (1.0 + jnp.exp(-x))


def _ln_kernel(x_ref, g_ref, b_ref, o_ref):
    o_ref[...] = _layer_norm(x_ref[...], g_ref[...], b_ref[...])


def _ln_call(x, g, b):
    n = x.shape[0]
    row = pl.BlockSpec((1, D_MODEL), lambda i: (0, 0))
    return pl.pallas_call(
        _ln_kernel,
        out_shape=jax.ShapeDtypeStruct((n, D_MODEL), F32),
        grid=(n // LN_TILE,),
        in_specs=[pl.BlockSpec((LN_TILE, D_MODEL), lambda i: (i, 0)), row, row],
        out_specs=pl.BlockSpec((LN_TILE, D_MODEL), lambda i: (i, 0)),
        compiler_params=_params(("parallel",)),
        name="ln_in",
    )(x, g.reshape(1, -1), b.reshape(1, -1))


def _inproj_kernel(h_ref, w_ref, wif_ref, p_ref, gif_ref):
    hb = h_ref[...].astype(BF16)
    p_ref[...] = jnp.dot(hb, w_ref[...], preferred_element_type=F32).astype(BF16)

    @pl.when(pl.program_id(1) == 0)
    def _():
        gif_ref[...] = jnp.dot(hb, wif_ref[...], preferred_element_type=F32)


def _inproj_call(h, w_main, w_if):
    n = h.shape[0]
    n_cols = w_main.shape[1]
    return pl.pallas_call(
        _inproj_kernel,
        out_shape=(jax.ShapeDtypeStruct((n, n_cols), BF16),
                   jax.ShapeDtypeStruct((n, LANES), F32)),
        grid=(n // PROJ_TILE_M, n_cols // PROJ_TILE_N),
        in_specs=[pl.BlockSpec((PROJ_TILE_M, D_MODEL), lambda i, j: (i, 0)),
                  pl.BlockSpec((D_MODEL, PROJ_TILE_N), lambda i, j: (0, j)),
                  pl.BlockSpec((D_MODEL, LANES), lambda i, j: (0, 0))],
        out_specs=(pl.BlockSpec((PROJ_TILE_M, PROJ_TILE_N), lambda i, j: (i, j)),
                   pl.BlockSpec((PROJ_TILE_M, LANES), lambda i, j: (i, 0))),
        compiler_params=_params(("parallel", "arbitrary")),
        name="in_proj",
    )(h, w_main, w_if)


def _log_sigmoid(x):
    return jnp.minimum(x, 0.0) - jnp.log1p(jnp.exp(-jnp.abs(x)))


def _mlstm_kernel(xm_ref, halo_ref, opre_ref, gcol_ref, grow_ref, convw_ref, convb_ref,
                  wq_ref, wk_ref, wv_ref, bcol_ref, brow_ref, ng_ref,
                  y_ref, xs_ref, c_ref, n_ref, m_ref):
    c = pl.program_id(0)
    t = M_CHUNK
    halo = BF16_SUBLANES

    @pl.when(c == 0)
    def _():
        c_ref[...] = jnp.zeros_like(c_ref)
        n_ref[...] = jnp.zeros_like(n_ref)
        m_ref[...] = jnp.zeros_like(m_ref)

    pos_col = c * t + lax.broadcasted_iota(jnp.int32, (t, 1), 0)
    valid_col = pos_col >= N_PAD
    xm = jnp.where(valid_col, xm_ref[...].astype(F32), 0.0)
    halo_pos = c * t - halo + lax.broadcasted_iota(jnp.int32, (halo, 1), 0)
    xs_ref[0:halo, :] = jnp.where(halo_pos >= N_PAD, halo_ref[...].astype(F32), 0.0)
    xs_ref[halo:halo + t, :] = xm
    conv = xm * convw_ref[CONV_W - 1:CONV_W, :] + convb_ref[...]
    for k in range(CONV_W - 1):
        off = halo - (CONV_W - 1) + k
        conv = conv + xs_ref[off:off + t, :] * convw_ref[k:k + 1, :]
    xc = conv * _sigmoid(conv)

    gc = gcol_ref[...] + bcol_ref[...]
    lane = lax.broadcasted_iota(jnp.int32, gc.shape, 1)
    lg_col = jnp.where(lane < M_HEADS,
                       jnp.where(valid_col, gc, NEG),
                       jnp.where(valid_col, _log_sigmoid(gc), 0.0))
    gr = grow_ref[...] + brow_ref[:, 0:1]
    valid_row = (c * t + lax.broadcasted_iota(jnp.int32, (1, t), 1)) >= N_PAD
    sub = lax.broadcasted_iota(jnp.int32, gr.shape, 0)
    lg_row = jnp.where(sub < M_HEADS,
                       jnp.where(valid_row, gr, NEG),
                       jnp.where(valid_row, _log_sigmoid(gr), 0.0))
    r_idx = lax.broadcasted_iota(jnp.int32, (t, t), 0)
    c_idx = lax.broadcasted_iota(jnp.int32, (t, t), 1)
    causal = c_idx <= r_idx
    tril = jnp.where(causal, 1.0, 0.0).astype(F32)
    triu = jnp.where(r_idx <= c_idx, 1.0, 0.0).astype(F32)
    cum_col = jnp.dot(tril, lg_col, preferred_element_type=F32, precision=lax.Precision.HIGHEST)
    cum_row = jnp.dot(lg_row, triu, preferred_element_type=F32, precision=lax.Precision.HIGHEST)

    nt_dims = (((1,), (1,)), ((), ()))
    tn_dims = (((0,), (0,)), ((), ()))
    for h in range(M_HEADS):
        sl = slice(h * M_HEAD_DIM, (h + 1) * M_HEAD_DIM)
        xch = xc[:, sl].astype(BF16)
        q = jnp.dot(xch, wq_ref[h], preferred_element_type=F32)
        k = jnp.dot(xch, wk_ref[h], preferred_element_type=F32) * (M_HEAD_DIM ** -0.5)
        v = jnp.dot(xm[:, sl].astype(BF16), wv_ref[h], preferred_element_type=F32)
        qb = q.astype(BF16)
        kb = k.astype(BF16)
        vb = v.astype(BF16)

        li_c = lg_col[:, h:h + 1]
        b_c = cum_col[:, M_HEADS + h:M_HEADS + h + 1]
        li_r = lg_row[h:h + 1, :]
        b_r = cum_row[M_HEADS + h:M_HEADS + h + 1, :]
        g = b_r[:, t - 1:t]
        m_prev = m_ref[h][:, 0:1]
        c_prev = c_ref[h]
        n_prev = n_ref[h]

        d_mat = jnp.where(causal, b_c - b_r + li_r, NEG)
        inter = b_c + m_prev
        m_t = jnp.maximum(inter, jnp.max(d_mat, axis=-1, keepdims=True))
        s = lax.dot_general(qb, kb, nt_dims, preferred_element_type=F32)
        w_intra = s * jnp.exp(d_mat - m_t)
        scale_inter = jnp.exp(inter - m_t)
        num = (jnp.dot(w_intra.astype(BF16), vb, preferred_element_type=F32)
               + scale_inter * jnp.dot(qb, c_prev.astype(BF16), preferred_element_type=F32))
        den = (jnp.sum(w_intra, axis=-1, keepdims=True)
               + scale_inter * jnp.sum(q * n_prev, axis=-1, keepdims=True))
        hh = num / jnp.maximum(jnp.abs(den), jnp.exp(-m_t))

        a = g - b_c + li_c
        m_new = jnp.maximum(g + m_prev, jnp.max(a, axis=0, keepdims=True))
        w_state = jnp.exp(a - m_new)
        decay = jnp.exp(g + m_prev - m_new)
        kw = k * w_state
        c_ref[h] = decay * c_prev + lax.dot_general(kw.astype(BF16), vb, tn_dims,
                                                    preferred_element_type=F32)
        n_ref[h] = decay * n_prev + jnp.sum(kw, axis=0, keepdims=True)
        m_ref[h] = jnp.broadcast_to(m_new, (1, LANES))

        hh = _sigmoid(opre_ref[:, sl].astype(F32)) * hh
        mu = jnp.mean(hh, axis=-1, keepdims=True)
        hc = hh - mu
        var = jnp.mean(hc * hc, axis=-1, keepdims=True)
        y_ref[:, sl] = (hc * lax.rsqrt(var + LN_EPS) * ng_ref[:, sl]).astype(BF16)


def _mlstm_call(p, gif, gif_t, conv_w, conv_b, wq, wk, wv, b_i, b_f, norm_g):
    n = p.shape[0]
    t = M_CHUNK
    halo = BF16_SUBLANES
    bias = jnp.concatenate([b_i, b_f]).astype(F32)
    bias_col = jnp.zeros((1, LANES), F32).at[0, :2 * M_HEADS].set(bias)
    bias_row = jnp.broadcast_to(bias[:, None], (2 * M_HEADS, LANES))
    const = lambda *shape: pl.BlockSpec(shape, lambda c: (0,) * len(shape))
    return pl.pallas_call(
        _mlstm_kernel,
        out_shape=jax.ShapeDtypeStruct((n, D_MODEL), BF16),
        grid=(n // t,),
        in_specs=[
            pl.BlockSpec((t, D_MODEL), lambda c: (c, COL_XM)),
            pl.BlockSpec((halo, D_MODEL),
                         lambda c: (jnp.maximum(c * (t // halo) - 1, 0), COL_XM)),
            pl.BlockSpec((t, D_MODEL), lambda c: (c, COL_OPRE)),
            pl.BlockSpec((t, LANES), lambda c: (c, 0)),
            pl.BlockSpec((2 * M_HEADS, t), lambda c: (0, c)),
            const(CONV_W, D_MODEL),
            const(1, D_MODEL),
            const(M_HEADS, M_HEAD_DIM, M_HEAD_DIM),
            const(M_HEADS, M_HEAD_DIM, M_HEAD_DIM),
            const(M_HEADS, M_HEAD_DIM, M_HEAD_DIM),
            const(1, LANES),
            const(2 * M_HEADS, LANES),
            const(1, D_MODEL),
        ],
        out_specs=pl.BlockSpec((t, D_MODEL), lambda c: (c, 0)),
        scratch_shapes=[
            pltpu.VMEM((t + halo, D_MODEL), F32),
            pltpu.VMEM((M_HEADS, M_HEAD_DIM, M_HEAD_DIM), F32),
            pltpu.VMEM((M_HEADS, 1, M_HEAD_DIM), F32),
            pltpu.VMEM((M_HEADS, 1, LANES), F32),
        ],
        compiler_params=_params(("arbitrary",)),
        name="mlstm",
    )(p, p, p, gif, gif_t, conv_w, conv_b.reshape(1, -1),
      wq.astype(BF16), wk.astype(BF16), wv.astype(BF16), bias_col, bias_row,
      norm_g.reshape(1, -1))


def _attn_kernel(q_ref, k_ref, v_ref, lq1_ref, lk1_ref, lq2_ref, lk2_ref, g_ref,
                 o_ref, qz_ref, m_ref, l_ref, acc_ref, *, lam_init):
    i = pl.program_id(1)
    tq, tk = ATTN_TQ, ATTN_TK

    q = q_ref[...] * jnp.asarray(A_HEAD_DIM ** -0.5, BF16)
    lane = lax.broadcasted_iota(jnp.int32, q.shape, 1)
    zero = jnp.zeros_like(q)
    qz_ref[0:tq, :] = jnp.where(lane < A_HEAD_DIM, q, zero)
    qz_ref[tq:2 * tq, :] = jnp.where(lane >= A_HEAD_DIM, q, zero)
    m_ref[...] = jnp.full_like(m_ref, NEG)
    l_ref[...] = jnp.zeros_like(l_ref)
    acc_ref[...] = jnp.zeros_like(acc_ref)

    nt_dims = (((1,), (1,)), ((), ()))

    def step(j, masked):
        start = pl.multiple_of(j * tk, tk)
        kt = k_ref[pl.ds(start, tk), :]
        vt = v_ref[pl.ds(start, tk), :]
        s = lax.dot_general(qz_ref[...], kt, nt_dims, preferred_element_type=F32)
        if masked:
            row = lax.broadcasted_iota(jnp.int32, (2 * tq, tk), 0)
            qpos = i * tq + jnp.where(row >= tq, row - tq, row)
            kpos = j * tk + lax.broadcasted_iota(jnp.int32, (2 * tq, tk), 1)
            s = jnp.where(kpos <= qpos, jnp.where(kpos >= N_PAD, s, NEG), NEG)
        m_prev = m_ref[...]
        m_new = jnp.maximum(m_prev, jnp.max(s, axis=-1, keepdims=True))
        alpha = jnp.exp(m_prev - m_new)
        p = jnp.exp(s - m_new)
        l_ref[...] = alpha * l_ref[...] + jnp.sum(p, axis=-1, keepdims=True)
        m_ref[...] = m_new
        pv = jnp.dot(p.astype(BF16), vt, preferred_element_type=F32)
        acc_ref[...] = alpha * acc_ref[...] + pv

    step(0, True)

    def body(j, carry):
        step(j, False)
        return carry

    lax.fori_loop(1, i, body, 0)

    @pl.when(i > 0)
    def _():
        step(i, True)

    lam = (jnp.exp(jnp.sum(lq1_ref[...] * lk1_ref[...], axis=-1, keepdims=True))
           - jnp.exp(jnp.sum(lq2_ref[...] * lk2_ref[...], axis=-1, keepdims=True))
           + lam_init)
    out = acc_ref[...] / l_ref[...]
    o = out[0:tq, :] - lam * out[tq:2 * tq, :]
    o = o * lax.rsqrt(jnp.mean(o * o, axis=-1, keepdims=True) + LN_EPS)
    o_ref[...] = (o * g_ref[...] * (1.0 - lam_init)).astype(BF16)


def _attn_call(p, lq1, lk1, lq2, lk2, norm_g, lam_init):
    n = p.shape[0]
    tq = ATTN_TQ
    hpb = D_MODEL // A_VDIM
    vec = lambda d: pl.BlockSpec((1, d), lambda h, i: (0, 0))
    return pl.pallas_call(
        functools.partial(_attn_kernel, lam_init=lam_init),
        out_shape=jax.ShapeDtypeStruct((n, D_MODEL), BF16),
        grid=(A_HEADS, n // tq),
        in_specs=[
            pl.BlockSpec((tq, A_VDIM), lambda h, i: (i, COL_Q * hpb + h)),
            pl.BlockSpec((n, A_VDIM), lambda h, i: (0, COL_K * hpb + h)),
            pl.BlockSpec((n, A_VDIM), lambda h, i: (0, COL_V * hpb + h)),
            vec(A_HEAD_DIM), vec(A_HEAD_DIM), vec(A_HEAD_DIM), vec(A_HEAD_DIM),
            vec(A_VDIM),
        ],
        out_specs=pl.BlockSpec((tq, A_VDIM), lambda h, i: (i, h)),
        scratch_shapes=[
            pltpu.VMEM((2 * tq, A_VDIM), BF16),
            pltpu.VMEM((2 * tq, 1), F32),
            pltpu.VMEM((2 * tq, 1), F32),
            pltpu.VMEM((2 * tq, A_VDIM), F32),
        ],
        compiler_params=_params(("parallel", "arbitrary")),
        name="diff_attn",
    )(p, p, p, lq1.reshape(1, -1), lk1.reshape(1, -1), lq2.reshape(1, -1),
      lk2.reshape(1, -1), norm_g.reshape(1, -1))


def _merge_kernel(ym_ref, ya_ref, ga_ref, gb_ref, h_ref, wm_ref, wa_ref, wo_ref,
                  g_ref, b_ref, wr_ref, rb_ref, h1_ref, comb_ref, *, alpha):
    bm = jnp.dot(ym_ref[...], wm_ref[...], preferred_element_type=F32)
    ba = jnp.dot(ya_ref[...], wa_ref[...], preferred_element_type=F32)
    merged = (_sigmoid(ga_ref[...].astype(F32)) * bm
              + _sigmoid(gb_ref[...].astype(F32)) * ba)
    proj = jnp.dot(merged.astype(BF16), wo_ref[...], preferred_element_type=F32)
    h1 = _layer_norm(alpha * h_ref[...] + proj, g_ref[...], b_ref[...])
    h1_ref[...] = h1

    logits = jnp.dot(h1, wr_ref[...], preferred_element_type=F32,
                     precision=lax.Precision.HIGHEST)
    lane = lax.broadcasted_iota(jnp.int32, logits.shape, 1)
    is_expert = lane < N_EXPERTS
    logits = jnp.where(is_expert, logits, NEG)
    ex = jnp.exp(logits - jnp.max(logits, axis=-1, keepdims=True))
    probs = ex / jnp.sum(ex, axis=-1, keepdims=True)
    sel = jnp.where(is_expert, probs + rb_ref[...], NEG)
    group = lane // EXPERTS_PER_GROUP
    best = jnp.max(jnp.where(group == 0, sel, NEG), axis=-1, keepdims=True)
    top_group = jnp.zeros_like(best, dtype=jnp.int32)
    for gidx in range(1, N_GROUPS):
        score = jnp.max(jnp.where(group == gidx, sel, NEG), axis=-1, keepdims=True)
        top_group = jnp.where(score > best, gidx, top_group)
        best = jnp.maximum(best, score)
    cand = jnp.where(group == top_group, sel, NEG)
    v1 = jnp.max(cand, axis=-1, keepdims=True)
    i1 = jnp.min(jnp.where(cand == v1, lane, LANES), axis=-1, keepdims=True)
    cand2 = jnp.where(lane == i1, NEG, cand)
    v2 = jnp.max(cand2, axis=-1, keepdims=True)
    i2 = jnp.min(jnp.where(cand2 == v2, lane, LANES), axis=-1, keepdims=True)
    p1 = jnp.sum(jnp.where(lane == i1, probs, 0.0), axis=-1, keepdims=True)
    p2 = jnp.sum(jnp.where(lane == i2, probs, 0.0), axis=-1, keepdims=True)
    tot = p1 + p2
    comb_ref[...] = (jnp.where(lane == i1, p1 / tot, 0.0)
                     + jnp.where(lane == i2, p2 / tot, 0.0))


def _merge_call(y_m, y_a, p, h, w_m, w_a, w_o, ln_g, ln_b, w_router, router_bias, alpha):
    n = h.shape[0]
    tm = MERGE_TILE
    full = pl.BlockSpec((D_MODEL, D_MODEL), lambda i: (0, 0))
    row = pl.BlockSpec((1, D_MODEL), lambda i: (0, 0))
    tile = lambda col: pl.BlockSpec((tm, D_MODEL), lambda i: (i, col))
    wr = jnp.zeros((D_MODEL, LANES), F32).at[:, :N_EXPERTS].set(w_router)
    rb = jnp.zeros((1, LANES), F32).at[0, :N_EXPERTS].set(router_bias)
    return pl.pallas_call(
        functools.partial(_merge_kernel, alpha=alpha),
        out_shape=(jax.ShapeDtypeStruct((n, D_MODEL), F32),
                   jax.ShapeDtypeStruct((n, LANES), F32)),
        grid=(n // tm,),
        in_specs=[tile(0), tile(0), tile(COL_GA), tile(COL_GB), tile(0),
                  full, full, full, row, row,
                  pl.BlockSpec((D_MODEL, LANES), lambda i: (0, 0)),
                  pl.BlockSpec((1, LANES), lambda i: (0, 0))],
        out_specs=(tile(0), pl.BlockSpec((tm, LANES), lambda i: (i, 0))),
        compiler_params=_params(("parallel",)),
        name="merge_router",
    )(y_m, y_a, p, p, h, w_m, w_a, w_o, ln_g.reshape(1, -1), ln_b.reshape(1, -1), wr, rb)


def _moe_kernel(h_ref, comb_ref, wg_ref, wu_ref, wd_ref, g_ref, b_ref, o_ref,
                xb_ref, acc_ref, *, alpha):
    e = pl.program_id(1)

    @pl.when(e == 0)
    def _():
        xb_ref[...] = h_ref[...].astype(BF16)
        acc_ref[...] = jnp.zeros_like(acc_ref)

    xb = xb_ref[...]
    gate = jnp.dot(xb, wg_ref[0], preferred_element_type=F32)
    up = jnp.dot(xb, wu_ref[0], preferred_element_type=F32)
    mid = (gate * _sigmoid(gate) * up).astype(BF16)
    out = jnp.dot(mid, wd_ref[0], preferred_element_type=F32)
    comb = comb_ref[...]
    lane = lax.broadcasted_iota(jnp.int32, comb.shape, 1)
    weight = jnp.sum(jnp.where(lane == e, comb, 0.0), axis=-1, keepdims=True)
    acc_ref[...] += weight * out

    @pl.when(e == N_EXPERTS - 1)
    def _():
        o_ref[...] = _layer_norm(alpha * h_ref[...] + acc_ref[...], g_ref[...], b_ref[...])


def _moe_call(h1, comb, w_gate, w_up, w_down, ln_g, ln_b, alpha):
    n = h1.shape[0]
    tm = MOE_TILE
    wspec = pl.BlockSpec((1, D_MODEL, D_MODEL), lambda i, e: (e, 0, 0))
    row = pl.BlockSpec((1, D_MODEL), lambda i, e: (0, 0))
    return pl.pallas_call(
        functools.partial(_moe_kernel, alpha=alpha),
        out_shape=jax.ShapeDtypeStruct((n, D_MODEL), F32),
        grid=(n // tm, N_EXPERTS),
        in_specs=[pl.BlockSpec((tm, D_MODEL), lambda i, e: (i, 0)),
                  pl.BlockSpec((tm, LANES), lambda i, e: (i, 0)),
                  wspec, wspec, wspec, row, row],
        out_specs=pl.BlockSpec((tm, D_MODEL), lambda i, e: (i, 0)),
        scratch_shapes=[pltpu.VMEM((tm, D_MODEL), BF16),
                        pltpu.VMEM((tm, D_MODEL), F32)],
        compiler_params=_params(("parallel", "arbitrary")),
        name="moe",
    )(h1, comb, w_gate, w_up, w_down, ln_g.reshape(1, -1), ln_b.reshape(1, -1))


def kernel(x, meta_tokens, ln_in_g, ln_in_b, w_in, conv_w, conv_b, wq_m, wk_m, wv_m, b_i, b_f, m_norm_g, lam_q1, lam_k1, lam_q2, lam_k2, a_norm_g, w_branch_m, w_branch_a, w_out, ln1_g, ln1_b, ln2_g, ln2_b, w_router, router_bias, w_gate_e, w_up_e, w_down_e):
    bsz, seq, dm = x.shape
    assert bsz == 1 and dm == D_MODEL
    depth = w_in.shape[0]
    alpha = float((2 * depth) ** 0.25)
    length = LEAD + seq
    n = -(-length // ROW_ALIGN) * ROW_ALIGN

    rows = jnp.concatenate([
        jnp.zeros((N_PAD, dm), F32), meta_tokens.astype(F32), x[0].astype(F32),
        jnp.zeros((n - length, dm), F32)], axis=0)
    h = _ln_call(rows, ln_in_g, ln_in_b)

    o_if = 2 * D_MODEL
    o_rest = o_if + 2 * M_HEADS
    for l in range(depth):
        w = w_in[l]
        w_main = jnp.concatenate([w[:, :o_if], w[:, o_rest:]], axis=1).astype(BF16)
        w_if = jnp.zeros((dm, LANES), F32).at[:, :2 * M_HEADS].set(w[:, o_if:o_rest]).astype(BF16)
        p, gif = _inproj_call(h, w_main, w_if)
        gif_t = gif[:, :2 * M_HEADS].T
        y_m = _mlstm_call(p, gif, gif_t, conv_w[l], conv_b[l], wq_m[l], wk_m[l], wv_m[l],
                          b_i[l], b_f[l], m_norm_g[l])
        lam_init = 0.8 - 0.6 * math.exp(-0.3 * l)
        y_a = _attn_call(p, lam_q1[l], lam_k1[l], lam_q2[l], lam_k2[l], a_norm_g[l], lam_init)
        h1, comb = _merge_call(y_m, y_a, p, h, w_branch_m[l].astype(BF16),
                               w_branch_a[l].astype(BF16), w_out[l].astype(BF16),
                               ln1_g[l], ln1_b[l], w_router, router_bias, alpha)
        h = _moe_call(h1, comb, w_gate_e[l].astype(BF16), w_up_e[l].astype(BF16),
                      w_down_e[l].astype(BF16), ln2_g[l], ln2_b[l], alpha)
    return h[LEAD:length][None].astype(x.dtype)
```

```python
import functools
import math

import jax
import jax.numpy as jnp
from jax import lax
from jax.experimental import pallas as pl
from jax.experimental.pallas import tpu as pltpu

F32 = jnp.float32
BF16 = jnp.bfloat16

D_MODEL = 1024
N_META = 16
LEAD = 128
N_PAD = LEAD - N_META
M_HEADS = 4
M_HEAD_DIM = D_MODEL // M_HEADS
CONV_W = 4
A_HEADS = 8
A_HEAD_DIM = D_MODEL // (2 * A_HEADS)
A_VDIM = 2 * A_HEAD_DIM
N_EXPERTS = 16
N_GROUPS = 4
EXPERTS_PER_GROUP = N_EXPERTS // N_GROUPS
LN_EPS = 1e-5
NEG = -1e30

LANES = 128
BF16_SUBLANES = 16
VMEM_LIMIT_BYTES = 56 * 1024 * 1024

ROW_ALIGN = 1536
LN_TILE = 512
PROJ_TILE_M = 1536
PROJ_TILE_N = 1024
M_CHUNK = 256
ATTN_TQ = 768
ATTN_TK = 768
MERGE_TILE = 512
MOE_TILE = 768

COL_XM, COL_OPRE, COL_Q, COL_K, COL_V, COL_GA, COL_GB = range(7)
N_PROJ_BLOCKS = 7


def _params(semantics):
    return pltpu.CompilerParams(dimension_semantics=semantics,
                                vmem_limit_bytes=VMEM_LIMIT_BYTES)


def _layer_norm(x, g, b):
    mu = jnp.mean(x, axis=-1, keepdims=True)
    xc = x - mu
    var = jnp.mean(xc * xc, axis=-1, keepdims=True)
    return xc * lax.rsqrt(var + LN_EPS) * g + b


def _sigmoid(x):
    return 1.0 / (1.0 + jnp.exp(-x))


def _ln_kernel(x_ref, g_ref, b_ref, o_ref):
    o_ref[...] = _layer_norm(x_ref[...], g_ref[...], b_ref[...])


def _ln_call(x, g, b):
    n = x.shape[0]
    row = pl.BlockSpec((1, D_MODEL), lambda i: (0, 0))
    return pl.pallas_call(
        _ln_kernel,
        out_shape=jax.ShapeDtypeStruct((n, D_MODEL), F32),
        grid=(n // LN_TILE,),
        in_specs=[pl.BlockSpec((LN_TILE, D_MODEL), lambda i: (i, 0)), row, row],
        out_specs=pl.BlockSpec((LN_TILE, D_MODEL), lambda i: (i, 0)),
        compiler_params=_params(("parallel",)),
        name="ln_in",
    )(x, g.reshape(1, -1), b.reshape(1, -1))


def _inproj_kernel(h_ref, w_ref, wif_ref, p_ref, gif_ref):
    hb = h_ref[...].astype(BF16)
    p_ref[...] = jnp.dot(hb, w_ref[...], preferred_element_type=F32).astype(BF16)

    @pl.when(pl.program_id(1) == 0)
    def _():
        gif_ref[...] = jnp.dot(hb, wif_ref[...], preferred_element_type=F32)


def _inproj_call(h, w_main, w_if):
    n = h.shape[0]
    n_cols = w_main.shape[1]
    return pl.pallas_call(
        _inproj_kernel,
        out_shape=(jax.ShapeDtypeStruct((n, n_cols), BF16),
                   jax.ShapeDtypeStruct((n, LANES), F32)),
        grid=(n // PROJ_TILE_M, n_cols // PROJ_TILE_N),
        in_specs=[pl.BlockSpec((PROJ_TILE_M, D_MODEL), lambda i, j: (i, 0)),
                  pl.BlockSpec((D_MODEL, PROJ_TILE_N), lambda i, j: (0, j)),
                  pl.BlockSpec((D_MODEL, LANES), lambda i, j: (0, 0))],
        out_specs=(pl.BlockSpec((PROJ_TILE_M, PROJ_TILE_N), lambda i, j: (i, j)),
                   pl.BlockSpec((PROJ_TILE_M, LANES), lambda i, j: (i, 0))),
        compiler_params=_params(("parallel", "arbitrary")),
        name="in_proj",
    )(h, w_main, w_if)


def _log_sigmoid(x):
    return jnp.minimum(x, 0.0) - jnp.log1p(jnp.exp(-jnp.abs(x)))


def _mlstm_kernel(xm_ref, halo_ref, opre_ref, gcol_ref, grow_ref, convw_ref, convb_ref,
                  wq_ref, wk_ref, wv_ref, bcol_ref, brow_ref, ng_ref,
                  y_ref, xs_ref, c_ref, n_ref, m_ref):
    c = pl.program_id(0)
    t = M_CHUNK
    halo = BF16_SUBLANES

    @pl.when(c == 0)
    def _():
        c_ref[...] = jnp.zeros_like(c_ref)
        n_ref[...] = jnp.zeros_like(n_ref)
        m_ref[...] = jnp.zeros_like(m_ref)

    pos_col = c * t + lax.broadcasted_iota(jnp.int32, (t, 1), 0)
    valid_col = pos_col >= N_PAD
    xm = jnp.where(valid_col, xm_ref[...].astype(F32), 0.0)
    halo_pos = c * t - halo + lax.broadcasted_iota(jnp.int32, (halo, 1), 0)
    xs_ref[0:halo, :] = jnp.where(halo_pos >= N_PAD, halo_ref[...].astype(F32), 0.0)
    xs_ref[halo:halo + t, :] = xm
    conv = xm * convw_ref[CONV_W - 1:CONV_W, :] + convb_ref[...]
    for k in range(CONV_W - 1):
        off = halo - (CONV_W - 1) + k
        conv = conv + xs_ref[off:off + t, :] * convw_ref[k:k + 1, :]
    xc = conv * _sigmoid(conv)

    gc = gcol_ref[...] + bcol_ref[...]
    lane = lax.broadcasted_iota(jnp.int32, gc.shape, 1)
    lg_col = jnp.where(lane < M_HEADS,
                       jnp.where(valid_col, gc, NEG),
                       jnp.where(valid_col, _log_sigmoid(gc), 0.0))
    gr = grow_ref[...] + brow_ref[:, 0:1]
    valid_row = (c * t + lax.broadcasted_iota(jnp.int32, (1, t), 1)) >= N_PAD
    sub = lax.broadcasted_iota(jnp.int32, gr.shape, 0)
    lg_row = jnp.where(sub < M_HEADS,
                       jnp.where(valid_row, gr, NEG),
                       jnp.where(valid_row, _log_sigmoid(gr), 0.0))
    r_idx = lax.broadcasted_iota(jnp.int32, (t, t), 0)
    c_idx = lax.broadcasted_iota(jnp.int32, (t, t), 1)
    causal = c_idx <= r_idx
    tril = jnp.where(causal, 1.0, 0.0).astype(F32)
    triu = jnp.where(r_idx <= c_idx, 1.0, 0.0).astype(F32)
    cum_col = jnp.dot(tril, lg_col, preferred_element_type=F32, precision=lax.Precision.HIGHEST)
    cum_row = jnp.dot(lg_row, triu, preferred_element_type=F32, precision=lax.Precision.HIGHEST)

    nt_dims = (((1,), (1,)), ((), ()))
    tn_dims = (((0,), (0,)), ((), ()))
    for h in range(M_HEADS):
        sl = slice(h * M_HEAD_DIM, (h + 1) * M_HEAD_DIM)
        xch = xc[:, sl].astype(BF16)
        q = jnp.dot(xch, wq_ref[h], preferred_element_type=F32)
        k = jnp.dot(xch, wk_ref[h], preferred_element_type=F32) * (M_HEAD_DIM ** -0.5)
        v = jnp.dot(xm[:, sl].astype(BF16), wv_ref[h], preferred_element_type=F32)
        qb = q.astype(BF16)
        kb = k.astype(BF16)
        vb = v.astype(BF16)

        li_c = lg_col[:, h:h + 1]
        b_c = cum_col[:, M_HEADS + h:M_HEADS + h + 1]
        li_r = lg_row[h:h + 1, :]
        b_r = cum_row[M_HEADS + h:M_HEADS + h + 1, :]
        g = b_r[:, t - 1:t]
        m_prev = m_ref[h][:, 0:1]
        c_prev = c_ref[h]
        n_prev = n_ref[h]

        d_mat = jnp.where(causal, b_c - b_r + li_r, NEG)
        inter = b_c + m_prev
        m_t = jnp.maximum(inter, jnp.max(d_mat, axis=-1, keepdims=True))
        s = lax.dot_general(qb, kb, nt_dims, preferred_element_type=F32)
        w_intra = s * jnp.exp(d_mat - m_t)
        scale_inter = jnp.exp(inter - m_t)
        num = (jnp.dot(w_intra.astype(BF16), vb, preferred_element_type=F32)
               + scale_inter * jnp.dot(qb, c_prev.astype(BF16), preferred_element_type=F32))
        den = (jnp.sum(w_intra, axis=-1, keepdims=True)
               + scale_inter * jnp.sum(q * n_prev, axis=-1, keepdims=True))
        hh = num / jnp.maximum(jnp.abs(den), jnp.exp(-m_t))

        a = g - b_c + li_c
        m_new = jnp.maximum(g + m_prev, jnp.max(a, axis=0, keepdims=True))
        w_state = jnp.exp(a - m_new)
        decay = jnp.exp(g + m_prev - m_new)
        kw = k * w_state
        c_ref[h] = decay * c_prev + lax.dot_general(kw.astype(BF16), vb, tn_dims,
                                                    preferred_element_type=F32)
        n_ref[h] = decay * n_prev + jnp.sum(kw, axis=0, keepdims=True)
        m_ref[h] = jnp.broadcast_to(m_new, (1, LANES))

        hh = _sigmoid(opre_ref[:, sl].astype(F32)) * hh
        mu = jnp.mean(hh, axis=-1, keepdims=True)
        hc = hh - mu
        var = jnp.mean(hc * hc, axis=-1, keepdims=True)
        y_ref[:, sl] = (hc * lax.rsqrt(var + LN_EPS) * ng_ref[:, sl]).astype(BF16)


def _mlstm_call(p, gif, gif_t, conv_w, conv_b, wq, wk, wv, b_i, b_f, norm_g):
    n = p.shape[0]
    t = M_CHUNK
    halo = BF16_SUBLANES
    bias = jnp.concatenate([b_i, b_f]).astype(F32)
    bias_col = jnp.zeros((1, LANES), F32).at[0, :2 * M_HEADS].set(bias)
    bias_row = jnp.broadcast_to(bias[:, None], (2 * M_HEADS, LANES))
    const = lambda *shape: pl.BlockSpec(shape, lambda c: (0,) * len(shape))
    return pl.pallas_call(
        _mlstm_kernel,
        out_shape=jax.ShapeDtypeStruct((n, D_MODEL), BF16),
        grid=(n // t,),
        in_specs=[
            pl.BlockSpec((t, D_MODEL), lambda c: (c, COL_XM)),
            pl.BlockSpec((halo, D_MODEL),
                         lambda c: (jnp.maximum(c * (t // halo) - 1, 0), COL_XM)),
            pl.BlockSpec((t, D_MODEL), lambda c: (c, COL_OPRE)),
            pl.BlockSpec((t, LANES), lambda c: (c, 0)),
            pl.BlockSpec((2 * M_HEADS, t), lambda c: (0, c)),
            const(CONV_W, D_MODEL),
            const(1, D_MODEL),
            const(M_HEADS, M_HEAD_DIM, M_HEAD_DIM),
            const(M_HEADS, M_HEAD_DIM, M_HEAD_DIM),
            const(M_HEADS, M_HEAD_DIM, M_HEAD_DIM),
            const(1, LANES),
            const(2 * M_HEADS, LANES),
            const(1, D_MODEL),
        ],
        out_specs=pl.BlockSpec((t, D_MODEL), lambda c: (c, 0)),
        scratch_shapes=[
            pltpu.VMEM((t + halo, D_MODEL), F32),
            pltpu.VMEM((M_HEADS, M_HEAD_DIM, M_HEAD_DIM), F32),
            pltpu.VMEM((M_HEADS, 1, M_HEAD_DIM), F32),
            pltpu.VMEM((M_HEADS, 1, LANES), F32),
        ],
        compiler_params=_params(("arbitrary",)),
        name="mlstm",
    )(p, p, p, gif, gif_t, conv_w, conv_b.reshape(1, -1),
      wq.astype(BF16), wk.astype(BF16), wv.astype(BF16), bias_col, bias_row,
      norm_g.reshape(1, -1))


def _attn_kernel(qt_ref, k_ref, vt_ref, lq1_ref, lk1_ref, lq2_ref, lk2_ref, g_ref,
                 o_ref, qz_ref, s0_ref, s1_ref, mt0_ref, mt1_ref, m_ref, l_ref, acc_ref,
                 *, lam_init):
    i = pl.program_id(1)
    tq, tk = ATTN_TQ, ATTN_TK
    s_refs = (s0_ref, s1_ref)
    mt_refs = (mt0_ref, mt1_ref)

    qt = qt_ref[0, 0]
    sub = lax.broadcasted_iota(jnp.int32, qt.shape, 0)
    zero = jnp.zeros_like(qt)
    qz_ref[:, 0:tq] = jnp.where(sub < A_HEAD_DIM, qt, zero)
    qz_ref[:, tq:2 * tq] = jnp.where(sub >= A_HEAD_DIM, qt, zero)
    m_ref[...] = jnp.full_like(m_ref, NEG)
    l_ref[...] = jnp.zeros_like(l_ref)
    acc_ref[...] = jnp.zeros_like(acc_ref)

    def scores(j, buf, masked):
        start = pl.multiple_of(j * tk, tk)
        kt = k_ref[pl.ds(start, tk), :]
        s = jnp.dot(kt, qz_ref[...], preferred_element_type=F32)
        if masked:
            kpos = j * tk + lax.broadcasted_iota(jnp.int32, (tk, 2 * tq), 0)
            col = lax.broadcasted_iota(jnp.int32, (tk, 2 * tq), 1)
            qpos = i * tq + jnp.where(col >= tq, col - tq, col)
            s = jnp.where(kpos <= qpos, jnp.where(kpos >= N_PAD, s, NEG), NEG)
        s_refs[buf][...] = s
        mt_refs[buf][...] = jnp.max(s, axis=0, keepdims=True)

    def accumulate(j, buf):
        m_prev = m_ref[...]
        m_new = jnp.maximum(m_prev, mt_refs[buf][...])
        alpha = jnp.exp2(m_prev - m_new)
        p = jnp.exp2(s_refs[buf][...] - m_new)
        l_ref[...] = alpha * l_ref[...] + jnp.sum(p, axis=0, keepdims=True)
        m_ref[...] = m_new
        pv = jnp.dot(vt_ref[0, j], p.astype(BF16), preferred_element_type=F32)
        acc_ref[...] = alpha * acc_ref[...] + pv

    def stage(j, masked_next):
        for buf in (0, 1):
            @pl.when(j % 2 == buf)
            def _():
                scores(j + 1, 1 - buf, masked_next)
                accumulate(j, buf)

    n_diag = tq // tk
    n_k = (i + 1) * n_diag
    first_diag = n_k - n_diag
    scores(0, 0, True)

    def body(j, carry):
        stage(j, False)
        return carry

    def body_masked(j, carry):
        stage(j, True)
        return carry

    lax.fori_loop(0, first_diag - 1, body, 0)
    lax.fori_loop(jnp.maximum(first_diag - 1, 0), n_k - 1, body_masked, 0)

    for buf in (0, 1):
        @pl.when((n_k - 1) % 2 == buf)
        def _():
            accumulate(n_k - 1, buf)

    lam = (jnp.exp(jnp.sum(lq1_ref[...] * lk1_ref[...], axis=-1, keepdims=True))
           - jnp.exp(jnp.sum(lq2_ref[...] * lk2_ref[...], axis=-1, keepdims=True))
           + lam_init)
    out = acc_ref[...] / l_ref[...]
    o = out[:, 0:tq] - lam * out[:, tq:2 * tq]
    o = o * lax.rsqrt(jnp.mean(o * o, axis=0, keepdims=True) + LN_EPS)
    o = o * g_ref[...] * (1.0 - lam_init)
    o_ref[...] = o.T.astype(BF16)


def _head_tiles_t(p, col, tile):
    n = p.shape[0]
    a = p[:, col * D_MODEL:(col + 1) * D_MODEL]
    return a.reshape(n // tile, tile, A_HEADS, A_VDIM).transpose(2, 0, 3, 1)


def _attn_call(p, lq1, lk1, lq2, lk2, norm_g, lam_init):
    n = p.shape[0]
    tq, tk = ATTN_TQ, ATTN_TK
    hpb = D_MODEL // A_VDIM
    vec = lambda d: pl.BlockSpec((1, d), lambda h, i: (0, 0))
    return pl.pallas_call(
        functools.partial(_attn_kernel, lam_init=lam_init),
        out_shape=jax.ShapeDtypeStruct((n, D_MODEL), BF16),
        grid=(A_HEADS, n // tq),
        in_specs=[
            pl.BlockSpec((1, 1, A_VDIM, tq), lambda h, i: (h, i, 0, 0)),
            pl.BlockSpec((n, A_VDIM), lambda h, i: (0, COL_K * hpb + h)),
            pl.BlockSpec((1, n // tk, A_VDIM, tk), lambda h, i: (h, 0, 0, 0)),
            vec(A_HEAD_DIM), vec(A_HEAD_DIM), vec(A_HEAD_DIM), vec(A_HEAD_DIM),
            pl.BlockSpec((A_VDIM, 1), lambda h, i: (0, 0)),
        ],
        out_specs=pl.BlockSpec((tq, A_VDIM), lambda h, i: (i, h)),
        scratch_shapes=[
            pltpu.VMEM((A_VDIM, 2 * tq), BF16),
            pltpu.VMEM((tk, 2 * tq), F32),
            pltpu.VMEM((tk, 2 * tq), F32),
            pltpu.VMEM((1, 2 * tq), F32),
            pltpu.VMEM((1, 2 * tq), F32),
            pltpu.VMEM((1, 2 * tq), F32),
            pltpu.VMEM((1, 2 * tq), F32),
            pltpu.VMEM((A_VDIM, 2 * tq), F32),
        ],
        compiler_params=_params(("parallel", "arbitrary")),
        name="diff_attn",
    )(_head_tiles_t(p, COL_Q, tq), p, _head_tiles_t(p, COL_V, tk),
      lq1.reshape(1, -1), lk1.reshape(1, -1), lq2.reshape(1, -1),
      lk2.reshape(1, -1), norm_g.reshape(-1, 1))


def _merge_kernel(ym_ref, ya_ref, ga_ref, gb_ref, h_ref, wm_ref, wa_ref, wo_ref,
                  g_ref, b_ref, wr_ref, rb_ref, h1_ref, comb_ref, *, alpha):
    bm = jnp.dot(ym_ref[...], wm_ref[...], preferred_element_type=F32)
    ba = jnp.dot(ya_ref[...], wa_ref[...], preferred_element_type=F32)
    merged = (_sigmoid(ga_ref[...].astype(F32)) * bm
              + _sigmoid(gb_ref[...].astype(F32)) * ba)
    proj = jnp.dot(merged.astype(BF16), wo_ref[...], preferred_element_type=F32)
    h1 = _layer_norm(alpha * h_ref[...] + proj, g_ref[...], b_ref[...])
    h1_ref[...] = h1

    logits = jnp.dot(h1, wr_ref[...], preferred_element_type=F32,
                     precision=lax.Precision.HIGHEST)
    lane = lax.broadcasted_iota(jnp.int32, logits.shape, 1)
    is_expert = lane < N_EXPERTS
    logits = jnp.where(is_expert, logits, NEG)
    ex = jnp.exp(logits - jnp.max(logits, axis=-1, keepdims=True))
    probs = ex / jnp.sum(ex, axis=-1, keepdims=True)
    sel = jnp.where(is_expert, probs + rb_ref[...], NEG)
    group = lane // EXPERTS_PER_GROUP
    best = jnp.max(jnp.where(group == 0, sel, NEG), axis=-1, keepdims=True)
    top_group = jnp.zeros_like(best, dtype=jnp.int32)
    for gidx in range(1, N_GROUPS):
        score = jnp.max(jnp.where(group == gidx, sel, NEG), axis=-1, keepdims=True)
        top_group = jnp.where(score > best, gidx, top_group)
        best = jnp.maximum(best, score)
    cand = jnp.where(group == top_group, sel, NEG)
    v1 = jnp.max(cand, axis=-1, keepdims=True)
    i1 = jnp.min(jnp.where(cand == v1, lane, LANES), axis=-1, keepdims=True)
    cand2 = jnp.where(lane == i1, NEG, cand)
    v2 = jnp.max(cand2, axis=-1, keepdims=True)
    i2 = jnp.min(jnp.where(cand2 == v2, lane, LANES), axis=-1, keepdims=True)
    p1 = jnp.sum(jnp.where(lane == i1, probs, 0.0), axis=-1, keepdims=True)
    p2 = jnp.sum(jnp.where(lane == i2, probs, 0.0), axis=-1, keepdims=True)
    tot = p1 + p2
    comb_ref[...] = (jnp.where(lane == i1, p1 / tot, 0.0)
                     + jnp.where(lane == i2, p2 / tot, 0.0))


def _merge_call(y_m, y_a, p, h, w_m, w_a, w_o, ln_g, ln_b, w_router, router_bias, alpha):
    n = h.shape[0]
    tm = MERGE_TILE
    full = pl.BlockSpec((D_MODEL, D_MODEL), lambda i: (0, 0))
    row = pl.BlockSpec((1, D_MODEL), lambda i: (0, 0))
    tile = lambda col: pl.BlockSpec((tm, D_MODEL), lambda i: (i, col))
    wr = jnp.zeros((D_MODEL, LANES), F32).at[:, :N_EXPERTS].set(w_router)
    rb = jnp.zeros((1, LANES), F32).at[0, :N_EXPERTS].set(router_bias)
    return pl.pallas_call(
        functools.partial(_merge_kernel, alpha=alpha),
        out_shape=(jax.ShapeDtypeStruct((n, D_MODEL), F32),
                   jax.ShapeDtypeStruct((n, LANES), F32)),
        grid=(n // tm,),
        in_specs=[tile(0), tile(0), tile(COL_GA), tile(COL_GB), tile(0),
                  full, full, full, row, row,
                  pl.BlockSpec((D_MODEL, LANES), lambda i: (0, 0)),
                  pl.BlockSpec((1, LANES), lambda i: (0, 0))],
        out_specs=(tile(0), pl.BlockSpec((tm, LANES), lambda i: (i, 0))),
        compiler_params=_params(("parallel",)),
        name="merge_router",
    )(y_m, y_a, p, p, h, w_m, w_a, w_o, ln_g.reshape(1, -1), ln_b.reshape(1, -1), wr, rb)


def _moe_kernel(h_ref, comb_ref, wg_ref, wu_ref, wd_ref, g_ref, b_ref, o_ref,
                xb_ref, acc_ref, *, alpha):
    e = pl.program_id(1)

    @pl.when(e == 0)
    def _():
        xb_ref[...] = h_ref[...].astype(BF16)
        acc_ref[...] = jnp.zeros_like(acc_ref)

    xb = xb_ref[...]
    gate = jnp.dot(xb, wg_ref[0], preferred_element_type=F32)
    up = jnp.dot(xb, wu_ref[0], preferred_element_type=F32)
    mid = (gate * _sigmoid(gate) * up).astype(BF16)
    out = jnp.dot(mid, wd_ref[0], preferred_element_type=F32)
    comb = comb_ref[...]
    lane = lax.broadcasted_iota(jnp.int32, comb.shape, 1)
    weight = jnp.sum(jnp.where(lane == e, comb, 0.0), axis=-1, keepdims=True)
    acc_ref[...] += weight * out

    @pl.when(e == N_EXPERTS - 1)
    def _():
        o_ref[...] = _layer_norm(alpha * h_ref[...] + acc_ref[...], g_ref[...], b_ref[...])


def _moe_call(h1, comb, w_gate, w_up, w_down, ln_g, ln_b, alpha):
    n = h1.shape[0]
    tm = MOE_TILE
    wspec = pl.BlockSpec((1, D_MODEL, D_MODEL), lambda i, e: (e, 0, 0))
    row = pl.BlockSpec((1, D_MODEL), lambda i, e: (0, 0))
    return pl.pallas_call(
        functools.partial(_moe_kernel, alpha=alpha),
        out_shape=jax.ShapeDtypeStruct((n, D_MODEL), F32),
        grid=(n // tm, N_EXPERTS),
        in_specs=[pl.BlockSpec((tm, D_MODEL), lambda i, e: (i, 0)),
                  pl.BlockSpec((tm, LANES), lambda i, e: (i, 0)),
                  wspec, wspec, wspec, row, row],
        out_specs=pl.BlockSpec((tm, D_MODEL), lambda i, e: (i, 0)),
        scratch_shapes=[pltpu.VMEM((tm, D_MODEL), BF16),
                        pltpu.VMEM((tm, D_MODEL), F32)],
        compiler_params=_params(("parallel", "arbitrary")),
        name="moe",
    )(h1, comb, w_gate, w_up, w_down, ln_g.reshape(1, -1), ln_b.reshape(1, -1))


def kernel(x, meta_tokens, ln_in_g, ln_in_b, w_in, conv_w, conv_b, wq_m, wk_m, wv_m, b_i, b_f, m_norm_g, lam_q1, lam_k1, lam_q2, lam_k2, a_norm_g, w_branch_m, w_branch_a, w_out, ln1_g, ln1_b, ln2_g, ln2_b, w_router, router_bias, w_gate_e, w_up_e, w_down_e):
    bsz, seq, dm = x.shape
    assert bsz == 1 and dm == D_MODEL
    depth = w_in.shape[0]
    alpha = float((2 * depth) ** 0.25)
    length = LEAD + seq
    n = -(-length // ROW_ALIGN) * ROW_ALIGN

    rows = jnp.concatenate([
        jnp.zeros((N_PAD, dm), F32), meta_tokens.astype(F32), x[0].astype(F32),
        jnp.zeros((n - length, dm), F32)], axis=0)
    h = _ln_call(rows, ln_in_g, ln_in_b)

    o_if = 2 * D_MODEL
    o_rest = o_if + 2 * M_HEADS
    for l in range(depth):
        w = w_in[l]
        q_scale = A_HEAD_DIM ** -0.5 * math.log2(math.e)
        w_main = jnp.concatenate(
            [w[:, :o_if], w[:, o_rest:o_rest + D_MODEL] * q_scale, w[:, o_rest + D_MODEL:]],
            axis=1).astype(BF16)
        w_if = jnp.zeros((dm, LANES), F32).at[:, :2 * M_HEADS].set(w[:, o_if:o_rest]).astype(BF16)
        p, gif = _inproj_call(h, w_main, w_if)
        gif_t = gif[:, :2 * M_HEADS].T
        y_m = _mlstm_call(p, gif, gif_t, conv_w[l], conv_b[l], wq_m[l], wk_m[l], wv_m[l],
                          b_i[l], b_f[l], m_norm_g[l])
        lam_init = 0.8 - 0.6 * math.exp(-0.3 * l)
        y_a = _attn_call(p, lam_q1[l], lam_k1[l], lam_q2[l], lam_k2[l], a_norm_g[l], lam_init)
        h1, comb = _merge_call(y_m, y_a, p, h, w_branch_m[l].astype(BF16),
                               w_branch_a[l].astype(BF16), w_out[l].astype(BF16),
                               ln1_g[l], ln1_b[l], w_router, router_bias, alpha)
        h = _moe_call(h1, comb, w_gate_e[l].astype(BF16), w_up_e[l].astype(BF16),
                      w_down_e[l].astype(BF16), ln2_g[l], ln2_b[l], alpha)
    return h[LEAD:length][None].astype(x.dtype)
```

```python
import functools
import math

import jax
import jax.numpy as jnp
from jax import lax
from jax.experimental import pallas as pl
from jax.experimental.pallas import tpu as pltpu

F32 = jnp.float32
BF16 = jnp.bfloat16

D_MODEL = 1024
N_META = 16
LEAD = 128
N_PAD = LEAD - N_META
M_HEADS = 4
M_HEAD_DIM = D_MODEL // M_HEADS
CONV_W = 4
A_HEADS = 8
A_HEAD_DIM = D_MODEL // (2 * A_HEADS)
A_VDIM = 2 * A_HEAD_DIM
N_EXPERTS = 16
N_GROUPS = 4
EXPERTS_PER_GROUP = N_EXPERTS // N_GROUPS
LN_EPS = 1e-5
NEG = -1e30

LANES = 128
BF16_SUBLANES = 16
VMEM_LIMIT_BYTES = 56 * 1024 * 1024

ROW_ALIGN = 1536
LN_TILE = 512
PROJ_TILE_M = 1536
PROJ_TILE_N = 1024
M_CHUNK = 256
ATTN_TQ = 768
ATTN_TK = 768
MERGE_TILE = 512
MOE_TILE = 1536
MOE_SLOTS = 256

COL_XM, COL_OPRE, COL_Q, COL_K, COL_V, COL_GA, COL_GB = range(7)
N_PROJ_BLOCKS = 7


def _params(semantics):
    return pltpu.CompilerParams(dimension_semantics=semantics,
                                vmem_limit_bytes=VMEM_LIMIT_BYTES)


def _layer_norm(x, g, b):
    mu = jnp.mean(x, axis=-1, keepdims=True)
    xc = x - mu
    var = jnp.mean(xc * xc, axis=-1, keepdims=True)
    return xc * lax.rsqrt(var + LN_EPS) * g + b


def _sigmoid(x):
    return 1.0 / (1.0 + jnp.exp(-x))


def _ln_kernel(x_ref, g_ref, b_ref, o_ref):
    o_ref[...] = _layer_norm(x_ref[...], g_ref[...], b_ref[...])


def _ln_call(x, g, b):
    n = x.shape[0]
    row = pl.BlockSpec((1, D_MODEL), lambda i: (0, 0))
    return pl.pallas_call(
        _ln_kernel,
        out_shape=jax.ShapeDtypeStruct((n, D_MODEL), F32),
        grid=(n // LN_TILE,),
        in_specs=[pl.BlockSpec((LN_TILE, D_MODEL), lambda i: (i, 0)), row, row],
        out_specs=pl.BlockSpec((LN_TILE, D_MODEL), lambda i: (i, 0)),
        compiler_params=_params(("parallel",)),
        name="ln_in",
    )(x, g.reshape(1, -1), b.reshape(1, -1))


def _inproj_kernel(h_ref, w_ref, wif_ref, p_ref, gif_ref):
    hb = h_ref[...].astype(BF16)
    p_ref[...] = jnp.dot(hb, w_ref[...], preferred_element_type=F32).astype(BF16)

    @pl.when(pl.program_id(1) == 0)
    def _():
        gif_ref[...] = jnp.dot(hb, wif_ref[...], preferred_element_type=F32)


def _inproj_call(h, w_main, w_if):
    n = h.shape[0]
    n_cols = w_main.shape[1]
    return pl.pallas_call(
        _inproj_kernel,
        out_shape=(jax.ShapeDtypeStruct((n, n_cols), BF16),
                   jax.ShapeDtypeStruct((n, LANES), F32)),
        grid=(n // PROJ_TILE_M, n_cols // PROJ_TILE_N),
        in_specs=[pl.BlockSpec((PROJ_TILE_M, D_MODEL), lambda i, j: (i, 0)),
                  pl.BlockSpec((D_MODEL, PROJ_TILE_N), lambda i, j: (0, j)),
                  pl.BlockSpec((D_MODEL, LANES), lambda i, j: (0, 0))],
        out_specs=(pl.BlockSpec((PROJ_TILE_M, PROJ_TILE_N), lambda i, j: (i, j)),
                   pl.BlockSpec((PROJ_TILE_M, LANES), lambda i, j: (i, 0))),
        compiler_params=_params(("parallel", "arbitrary")),
        name="in_proj",
    )(h, w_main, w_if)


def _log_sigmoid(x):
    return jnp.minimum(x, 0.0) - jnp.log1p(jnp.exp(-jnp.abs(x)))


def _mlstm_kernel(xm_ref, halo_ref, opre_ref, gcol_ref, grow_ref, convw_ref, convb_ref,
                  wq_ref, wk_ref, wv_ref, bcol_ref, brow_ref, ng_ref,
                  y_ref, xs_ref, c_ref, n_ref, m_ref):
    c = pl.program_id(0)
    t = M_CHUNK
    halo = BF16_SUBLANES

    @pl.when(c == 0)
    def _():
        c_ref[...] = jnp.zeros_like(c_ref)
        n_ref[...] = jnp.zeros_like(n_ref)
        m_ref[...] = jnp.zeros_like(m_ref)

    pos_col = c * t + lax.broadcasted_iota(jnp.int32, (t, 1), 0)
    valid_col = pos_col >= N_PAD
    xm = jnp.where(valid_col, xm_ref[...].astype(F32), 0.0)
    halo_pos = c * t - halo + lax.broadcasted_iota(jnp.int32, (halo, 1), 0)
    xs_ref[0:halo, :] = jnp.where(halo_pos >= N_PAD, halo_ref[...].astype(F32), 0.0)
    xs_ref[halo:halo + t, :] = xm
    conv = xm * convw_ref[CONV_W - 1:CONV_W, :] + convb_ref[...]
    for k in range(CONV_W - 1):
        off = halo - (CONV_W - 1) + k
        conv = conv + xs_ref[off:off + t, :] * convw_ref[k:k + 1, :]
    xc = conv * _sigmoid(conv)

    gc = gcol_ref[...] + bcol_ref[...]
    lane = lax.broadcasted_iota(jnp.int32, gc.shape, 1)
    lg_col = jnp.where(lane < M_HEADS,
                       jnp.where(valid_col, gc, NEG),
                       jnp.where(valid_col, _log_sigmoid(gc), 0.0))
    gr = grow_ref[...] + brow_ref[:, 0:1]
    valid_row = (c * t + lax.broadcasted_iota(jnp.int32, (1, t), 1)) >= N_PAD
    sub = lax.broadcasted_iota(jnp.int32, gr.shape, 0)
    lg_row = jnp.where(sub < M_HEADS,
                       jnp.where(valid_row, gr, NEG),
                       jnp.where(valid_row, _log_sigmoid(gr), 0.0))
    r_idx = lax.broadcasted_iota(jnp.int32, (t, t), 0)
    c_idx = lax.broadcasted_iota(jnp.int32, (t, t), 1)
    causal = c_idx <= r_idx
    tril = jnp.where(causal, 1.0, 0.0).astype(F32)
    triu = jnp.where(r_idx <= c_idx, 1.0, 0.0).astype(F32)
    cum_col = jnp.dot(tril, lg_col, preferred_element_type=F32, precision=lax.Precision.HIGHEST)
    cum_row = jnp.dot(lg_row, triu, preferred_element_type=F32, precision=lax.Precision.HIGHEST)

    nt_dims = (((1,), (1,)), ((), ()))
    tn_dims = (((0,), (0,)), ((), ()))
    for h in range(M_HEADS):
        sl = slice(h * M_HEAD_DIM, (h + 1) * M_HEAD_DIM)
        xch = xc[:, sl].astype(BF16)
        q = jnp.dot(xch, wq_ref[h], preferred_element_type=F32)
        k = jnp.dot(xch, wk_ref[h], preferred_element_type=F32) * (M_HEAD_DIM ** -0.5)
        v = jnp.dot(xm[:, sl].astype(BF16), wv_ref[h], preferred_element_type=F32)
        qb = q.astype(BF16)
        kb = k.astype(BF16)
        vb = v.astype(BF16)

        li_c = lg_col[:, h:h + 1]
        b_c = cum_col[:, M_HEADS + h:M_HEADS + h + 1]
        li_r = lg_row[h:h + 1, :]
        b_r = cum_row[M_HEADS + h:M_HEADS + h + 1, :]
        g = b_r[:, t - 1:t]
        m_prev = m_ref[h][:, 0:1]
        c_prev = c_ref[h]
        n_prev = n_ref[h]

        d_mat = jnp.where(causal, b_c - b_r + li_r, NEG)
        inter = b_c + m_prev
        m_t = jnp.maximum(inter, jnp.max(d_mat, axis=-1, keepdims=True))
        s = lax.dot_general(qb, kb, nt_dims, preferred_element_type=F32)
        w_intra = s * jnp.exp(d_mat - m_t)
        scale_inter = jnp.exp(inter - m_t)
        num = (jnp.dot(w_intra.astype(BF16), vb, preferred_element_type=F32)
               + scale_inter * jnp.dot(qb, c_prev.astype(BF16), preferred_element_type=F32))
        den = (jnp.sum(w_intra, axis=-1, keepdims=True)
               + scale_inter * jnp.sum(q * n_prev, axis=-1, keepdims=True))
        hh = num / jnp.maximum(jnp.abs(den), jnp.exp(-m_t))

        a = g - b_c + li_c
        m_new = jnp.maximum(g + m_prev, jnp.max(a, axis=0, keepdims=True))
        w_state = jnp.exp(a - m_new)
        decay = jnp.exp(g + m_prev - m_new)
        kw = k * w_state
        c_ref[h] = decay * c_prev + lax.dot_general(kw.astype(BF16), vb, tn_dims,
                                                    preferred_element_type=F32)
        n_ref[h] = decay * n_prev + jnp.sum(kw, axis=0, keepdims=True)
        m_ref[h] = jnp.broadcast_to(m_new, (1, LANES))

        hh = _sigmoid(opre_ref[:, sl].astype(F32)) * hh
        mu = jnp.mean(hh, axis=-1, keepdims=True)
        hc = hh - mu
        var = jnp.mean(hc * hc, axis=-1, keepdims=True)
        y_ref[:, sl] = (hc * lax.rsqrt(var + LN_EPS) * ng_ref[:, sl]).astype(BF16)


def _mlstm_call(p, gif, gif_t, conv_w, conv_b, wq, wk, wv, b_i, b_f, norm_g):
    n = p.shape[0]
    t = M_CHUNK
    halo = BF16_SUBLANES
    bias = jnp.concatenate([b_i, b_f]).astype(F32)
    bias_col = jnp.zeros((1, LANES), F32).at[0, :2 * M_HEADS].set(bias)
    bias_row = jnp.broadcast_to(bias[:, None], (2 * M_HEADS, LANES))
    const = lambda *shape: pl.BlockSpec(shape, lambda c: (0,) * len(shape))
    return pl.pallas_call(
        _mlstm_kernel,
        out_shape=jax.ShapeDtypeStruct((n, D_MODEL), BF16),
        grid=(n // t,),
        in_specs=[
            pl.BlockSpec((t, D_MODEL), lambda c: (c, COL_XM)),
            pl.BlockSpec((halo, D_MODEL),
                         lambda c: (jnp.maximum(c * (t // halo) - 1, 0), COL_XM)),
            pl.BlockSpec((t, D_MODEL), lambda c: (c, COL_OPRE)),
            pl.BlockSpec((t, LANES), lambda c: (c, 0)),
            pl.BlockSpec((2 * M_HEADS, t), lambda c: (0, c)),
            const(CONV_W, D_MODEL),
            const(1, D_MODEL),
            const(M_HEADS, M_HEAD_DIM, M_HEAD_DIM),
            const(M_HEADS, M_HEAD_DIM, M_HEAD_DIM),
            const(M_HEADS, M_HEAD_DIM, M_HEAD_DIM),
            const(1, LANES),
            const(2 * M_HEADS, LANES),
            const(1, D_MODEL),
        ],
        out_specs=pl.BlockSpec((t, D_MODEL), lambda c: (c, 0)),
        scratch_shapes=[
            pltpu.VMEM((t + halo, D_MODEL), F32),
            pltpu.VMEM((M_HEADS, M_HEAD_DIM, M_HEAD_DIM), F32),
            pltpu.VMEM((M_HEADS, 1, M_HEAD_DIM), F32),
            pltpu.VMEM((M_HEADS, 1, LANES), F32),
        ],
        compiler_params=_params(("arbitrary",)),
        name="mlstm",
    )(p, p, p, gif, gif_t, conv_w, conv_b.reshape(1, -1),
      wq.astype(BF16), wk.astype(BF16), wv.astype(BF16), bias_col, bias_row,
      norm_g.reshape(1, -1))


def _attn_kernel(qt_ref, k_ref, vt_ref, lq1_ref, lk1_ref, lq2_ref, lk2_ref, g_ref,
                 o_ref, qz_ref, s0_ref, s1_ref, mt0_ref, mt1_ref, m_ref, l_ref, acc_ref,
                 *, lam_init):
    i = pl.program_id(1)
    tq, tk = ATTN_TQ, ATTN_TK
    s_refs = (s0_ref, s1_ref)
    mt_refs = (mt0_ref, mt1_ref)

    qt = qt_ref[0, 0]
    sub = lax.broadcasted_iota(jnp.int32, qt.shape, 0)
    zero = jnp.zeros_like(qt)
    qz_ref[:, 0:tq] = jnp.where(sub < A_HEAD_DIM, qt, zero)
    qz_ref[:, tq:2 * tq] = jnp.where(sub >= A_HEAD_DIM, qt, zero)
    m_ref[...] = jnp.full_like(m_ref, NEG)
    l_ref[...] = jnp.zeros_like(l_ref)
    acc_ref[...] = jnp.zeros_like(acc_ref)

    def scores(j, buf, masked):
        start = pl.multiple_of(j * tk, tk)
        kt = k_ref[pl.ds(start, tk), :]
        s = jnp.dot(kt, qz_ref[...], preferred_element_type=F32)
        if masked:
            kpos = j * tk + lax.broadcasted_iota(jnp.int32, (tk, 2 * tq), 0)
            col = lax.broadcasted_iota(jnp.int32, (tk, 2 * tq), 1)
            qpos = i * tq + jnp.where(col >= tq, col - tq, col)
            s = jnp.where(kpos <= qpos, jnp.where(kpos >= N_PAD, s, NEG), NEG)
        s_refs[buf][...] = s
        mt_refs[buf][...] = jnp.max(s, axis=0, keepdims=True)

    def accumulate(j, buf):
        m_prev = m_ref[...]
        m_new = jnp.maximum(m_prev, mt_refs[buf][...])
        alpha = jnp.exp2(m_prev - m_new)
        p = jnp.exp2(s_refs[buf][...] - m_new)
        l_ref[...] = alpha * l_ref[...] + jnp.sum(p, axis=0, keepdims=True)
        m_ref[...] = m_new
        pv = jnp.dot(vt_ref[0, j], p.astype(BF16), preferred_element_type=F32)
        acc_ref[...] = alpha * acc_ref[...] + pv

    def stage(j, masked_next):
        for buf in (0, 1):
            @pl.when(j % 2 == buf)
            def _():
                scores(j + 1, 1 - buf, masked_next)
                accumulate(j, buf)

    n_diag = tq // tk
    n_k = (i + 1) * n_diag
    first_diag = n_k - n_diag
    scores(0, 0, True)

    def body(j, carry):
        stage(j, False)
        return carry

    def body_masked(j, carry):
        stage(j, True)
        return carry

    lax.fori_loop(0, first_diag - 1, body, 0)
    lax.fori_loop(jnp.maximum(first_diag - 1, 0), n_k - 1, body_masked, 0)

    for buf in (0, 1):
        @pl.when((n_k - 1) % 2 == buf)
        def _():
            accumulate(n_k - 1, buf)

    lam = (jnp.exp(jnp.sum(lq1_ref[...] * lk1_ref[...], axis=-1, keepdims=True))
           - jnp.exp(jnp.sum(lq2_ref[...] * lk2_ref[...], axis=-1, keepdims=True))
           + lam_init)
    out = acc_ref[...] / l_ref[...]
    o = out[:, 0:tq] - lam * out[:, tq:2 * tq]
    o = o * lax.rsqrt(jnp.mean(o * o, axis=0, keepdims=True) + LN_EPS)
    o = o * g_ref[...] * (1.0 - lam_init)
    o_ref[...] = o.T.astype(BF16)


def _head_tiles_t(p, col, tile):
    n = p.shape[0]
    a = p[:, col * D_MODEL:(col + 1) * D_MODEL]
    return a.reshape(n // tile, tile, A_HEADS, A_VDIM).transpose(2, 0, 3, 1)


def _attn_call(p, lq1, lk1, lq2, lk2, norm_g, lam_init):
    n = p.shape[0]
    tq, tk = ATTN_TQ, ATTN_TK
    hpb = D_MODEL // A_VDIM
    vec = lambda d: pl.BlockSpec((1, d), lambda h, i: (0, 0))
    return pl.pallas_call(
        functools.partial(_attn_kernel, lam_init=lam_init),
        out_shape=jax.ShapeDtypeStruct((n, D_MODEL), BF16),
        grid=(A_HEADS, n // tq),
        in_specs=[
            pl.BlockSpec((1, 1, A_VDIM, tq), lambda h, i: (h, i, 0, 0)),
            pl.BlockSpec((n, A_VDIM), lambda h, i: (0, COL_K * hpb + h)),
            pl.BlockSpec((1, n // tk, A_VDIM, tk), lambda h, i: (h, 0, 0, 0)),
            vec(A_HEAD_DIM), vec(A_HEAD_DIM), vec(A_HEAD_DIM), vec(A_HEAD_DIM),
            pl.BlockSpec((A_VDIM, 1), lambda h, i: (0, 0)),
        ],
        out_specs=pl.BlockSpec((tq, A_VDIM), lambda h, i: (i, h)),
        scratch_shapes=[
            pltpu.VMEM((A_VDIM, 2 * tq), BF16),
            pltpu.VMEM((tk, 2 * tq), F32),
            pltpu.VMEM((tk, 2 * tq), F32),
            pltpu.VMEM((1, 2 * tq), F32),
            pltpu.VMEM((1, 2 * tq), F32),
            pltpu.VMEM((1, 2 * tq), F32),
            pltpu.VMEM((1, 2 * tq), F32),
            pltpu.VMEM((A_VDIM, 2 * tq), F32),
        ],
        compiler_params=_params(("parallel", "arbitrary")),
        name="diff_attn",
    )(_head_tiles_t(p, COL_Q, tq), p, _head_tiles_t(p, COL_V, tk),
      lq1.reshape(1, -1), lk1.reshape(1, -1), lq2.reshape(1, -1),
      lk2.reshape(1, -1), norm_g.reshape(-1, 1))


def _merge_kernel(ym_ref, ya_ref, ga_ref, gb_ref, h_ref, wm_ref, wa_ref, wo_ref,
                  g_ref, b_ref, wr_ref, rb_ref, h1_ref, comb_ref, *, alpha):
    bm = jnp.dot(ym_ref[...], wm_ref[...], preferred_element_type=F32)
    ba = jnp.dot(ya_ref[...], wa_ref[...], preferred_element_type=F32)
    merged = (_sigmoid(ga_ref[...].astype(F32)) * bm
              + _sigmoid(gb_ref[...].astype(F32)) * ba)
    proj = jnp.dot(merged.astype(BF16), wo_ref[...], preferred_element_type=F32)
    h1 = _layer_norm(alpha * h_ref[...] + proj, g_ref[...], b_ref[...])
    h1_ref[...] = h1

    logits = jnp.dot(h1, wr_ref[...], preferred_element_type=F32,
                     precision=lax.Precision.HIGHEST)
    lane = lax.broadcasted_iota(jnp.int32, logits.shape, 1)
    is_expert = lane < N_EXPERTS
    logits = jnp.where(is_expert, logits, NEG)
    ex = jnp.exp(logits - jnp.max(logits, axis=-1, keepdims=True))
    probs = ex / jnp.sum(ex, axis=-1, keepdims=True)
    sel = jnp.where(is_expert, probs + rb_ref[...], NEG)
    group = lane // EXPERTS_PER_GROUP
    best = jnp.max(jnp.where(group == 0, sel, NEG), axis=-1, keepdims=True)
    top_group = jnp.zeros_like(best, dtype=jnp.int32)
    for gidx in range(1, N_GROUPS):
        score = jnp.max(jnp.where(group == gidx, sel, NEG), axis=-1, keepdims=True)
        top_group = jnp.where(score > best, gidx, top_group)
        best = jnp.maximum(best, score)
    cand = jnp.where(group == top_group, sel, NEG)
    v1 = jnp.max(cand, axis=-1, keepdims=True)
    i1 = jnp.min(jnp.where(cand == v1, lane, LANES), axis=-1, keepdims=True)
    cand2 = jnp.where(lane == i1, NEG, cand)
    v2 = jnp.max(cand2, axis=-1, keepdims=True)
    i2 = jnp.min(jnp.where(cand2 == v2, lane, LANES), axis=-1, keepdims=True)
    p1 = jnp.sum(jnp.where(lane == i1, probs, 0.0), axis=-1, keepdims=True)
    p2 = jnp.sum(jnp.where(lane == i2, probs, 0.0), axis=-1, keepdims=True)
    tot = p1 + p2
    comb_ref[...] = (jnp.where(lane == i1, p1 / tot, 0.0)
                     + jnp.where(lane == i2, p2 / tot, 0.0))


def _merge_call(y_m, y_a, p, h, w_m, w_a, w_o, ln_g, ln_b, w_router, router_bias, alpha):
    n = h.shape[0]
    tm = MERGE_TILE
    full = pl.BlockSpec((D_MODEL, D_MODEL), lambda i: (0, 0))
    row = pl.BlockSpec((1, D_MODEL), lambda i: (0, 0))
    tile = lambda col: pl.BlockSpec((tm, D_MODEL), lambda i: (i, col))
    wr = jnp.zeros((D_MODEL, LANES), F32).at[:, :N_EXPERTS].set(w_router)
    rb = jnp.zeros((1, LANES), F32).at[0, :N_EXPERTS].set(router_bias)
    return pl.pallas_call(
        functools.partial(_merge_kernel, alpha=alpha),
        out_shape=(jax.ShapeDtypeStruct((n, D_MODEL), F32),
                   jax.ShapeDtypeStruct((n, LANES), F32)),
        grid=(n // tm,),
        in_specs=[tile(0), tile(0), tile(COL_GA), tile(COL_GB), tile(0),
                  full, full, full, row, row,
                  pl.BlockSpec((D_MODEL, LANES), lambda i: (0, 0)),
                  pl.BlockSpec((1, LANES), lambda i: (0, 0))],
        out_specs=(tile(0), pl.BlockSpec((tm, LANES), lambda i: (i, 0))),
        compiler_params=_params(("parallel",)),
        name="merge_router",
    )(y_m, y_a, p, p, h, w_m, w_a, w_o, ln_g.reshape(1, -1), ln_b.reshape(1, -1), wr, rb)


def _moe_kernel(h_ref, comb_ref, combt_ref, wg_ref, wu_ref, wd_ref, g_ref, b_ref, o_ref,
                xb_ref, acc_ref, rankc_ref, rankr_ref, whl_ref, *, alpha):
    e = pl.program_id(1)
    tm = MOE_TILE
    ns = MOE_SLOTS

    @pl.when(e == 0)
    def _():
        xb_ref[...] = h_ref[...].astype(BF16)
        acc_ref[...] = jnp.zeros_like(acc_ref)
        comb = comb_ref[...]
        r_idx = lax.broadcasted_iota(jnp.int32, (tm, tm), 0)
        c_idx = lax.broadcasted_iota(jnp.int32, (tm, tm), 1)
        before = jnp.where(c_idx < r_idx, 1.0, 0.0).astype(BF16)
        rankc_ref[...] = jnp.dot(before, jnp.where(comb > 0.0, 1.0, 0.0).astype(BF16),
                                 preferred_element_type=F32)
        after = jnp.where(r_idx < c_idx, 1.0, 0.0).astype(BF16)
        rankr_ref[...] = jnp.dot(jnp.where(combt_ref[...] > 0.0, 1.0, 0.0).astype(BF16), after,
                                 preferred_element_type=F32)
        hi = comb.astype(BF16).astype(F32)
        lo = comb - hi
        whl_ref[...] = (hi + pltpu.roll(lo, N_EXPERTS, 1)).astype(BF16)

    comb = comb_ref[...]
    lane = lax.broadcasted_iota(jnp.int32, comb.shape, 1)
    mine = lane == e
    routed_c = jnp.sum(jnp.where(mine, comb, 0.0), axis=-1, keepdims=True) > 0.0
    rank_c = jnp.sum(jnp.where(mine, rankc_ref[...], 0.0), axis=-1, keepdims=True)
    routed_r = combt_ref[pl.ds(e, 1), :] > 0.0
    rank_r = rankr_ref[pl.ds(e, 1), :]
    count = jnp.sum(jnp.where(routed_r, 1.0, 0.0)).astype(jnp.int32)
    n_blocks = (count + ns - 1) // ns

    def block(b, carry):
        base = (b * ns).astype(F32)
        slot_r = lax.broadcasted_iota(jnp.int32, (ns, tm), 0).astype(F32) + base
        sel = jnp.where(routed_r & (rank_r == slot_r), 1.0, 0.0).astype(BF16)
        slot_c = lax.broadcasted_iota(jnp.int32, (tm, ns), 1).astype(F32) + base
        sel_t = jnp.where(routed_c & (rank_c == slot_c), 1.0, 0.0).astype(BF16)
        xs = jnp.dot(sel, xb_ref[...], preferred_element_type=F32).astype(BF16)
        ws = jnp.dot(sel, whl_ref[...], preferred_element_type=F32)
        lane_s = lax.broadcasted_iota(jnp.int32, ws.shape, 1)
        w_slot = jnp.sum(jnp.where((lane_s == e) | (lane_s == e + N_EXPERTS), ws, 0.0),
                         axis=-1, keepdims=True)
        gate = jnp.dot(xs, wg_ref[0], preferred_element_type=F32)
        up = jnp.dot(xs, wu_ref[0], preferred_element_type=F32)
        mid = (gate * _sigmoid(gate) * up).astype(BF16)
        out = jnp.dot(mid, wd_ref[0], preferred_element_type=F32) * w_slot
        acc_ref[...] += jnp.dot(sel_t, out.astype(BF16), preferred_element_type=F32)
        return carry

    lax.fori_loop(0, n_blocks, block, 0)

    @pl.when(e == N_EXPERTS - 1)
    def _():
        o_ref[...] = _layer_norm(alpha * h_ref[...] + acc_ref[...], g_ref[...], b_ref[...])


def _moe_call(h1, comb, w_gate, w_up, w_down, ln_g, ln_b, alpha):
    n = h1.shape[0]
    tm = MOE_TILE
    comb_t = comb[:, :N_EXPERTS].T
    wspec = pl.BlockSpec((1, D_MODEL, D_MODEL), lambda i, e: (e, 0, 0))
    row = pl.BlockSpec((1, D_MODEL), lambda i, e: (0, 0))
    return pl.pallas_call(
        functools.partial(_moe_kernel, alpha=alpha),
        out_shape=jax.ShapeDtypeStruct((n, D_MODEL), F32),
        grid=(n // tm, N_EXPERTS),
        in_specs=[pl.BlockSpec((tm, D_MODEL), lambda i, e: (i, 0),
                               pipeline_mode=pl.Buffered(1)),
                  pl.BlockSpec((tm, LANES), lambda i, e: (i, 0)),
                  pl.BlockSpec((N_EXPERTS, tm), lambda i, e: (0, i)),
                  wspec, wspec, wspec, row, row],
        out_specs=pl.BlockSpec((tm, D_MODEL), lambda i, e: (i, 0)),
        scratch_shapes=[pltpu.VMEM((tm, D_MODEL), BF16),
                        pltpu.VMEM((tm, D_MODEL), F32),
                        pltpu.VMEM((tm, LANES), F32),
                        pltpu.VMEM((N_EXPERTS, tm), F32),
                        pltpu.VMEM((tm, LANES), BF16)],
        compiler_params=_params(("parallel", "arbitrary")),
        name="moe",
    )(h1, comb, comb_t, w_gate, w_up, w_down, ln_g.reshape(1, -1), ln_b.reshape(1, -1))


def kernel(x, meta_tokens, ln_in_g, ln_in_b, w_in, conv_w, conv_b, wq_m, wk_m, wv_m, b_i, b_f, m_norm_g, lam_q1, lam_k1, lam_q2, lam_k2, a_norm_g, w_branch_m, w_branch_a, w_out, ln1_g, ln1_b, ln2_g, ln2_b, w_router, router_bias, w_gate_e, w_up_e, w_down_e):
    bsz, seq, dm = x.shape
    assert bsz == 1 and dm == D_MODEL
    depth = w_in.shape[0]
    alpha = float((2 * depth) ** 0.25)
    length = LEAD + seq
    n = -(-length // ROW_ALIGN) * ROW_ALIGN

    rows = jnp.concatenate([
        jnp.zeros((N_PAD, dm), F32), meta_tokens.astype(F32), x[0].astype(F32),
        jnp.zeros((n - length, dm), F32)], axis=0)
    h = _ln_call(rows, ln_in_g, ln_in_b)

    o_if = 2 * D_MODEL
    o_rest = o_if + 2 * M_HEADS
    for l in range(depth):
        w = w_in[l]
        q_scale = A_HEAD_DIM ** -0.5 * math.log2(math.e)
        w_main = jnp.concatenate(
            [w[:, :o_if], w[:, o_rest:o_rest + D_MODEL] * q_scale, w[:, o_rest + D_MODEL:]],
            axis=1).astype(BF16)
        w_if = jnp.zeros((dm, LANES), F32).at[:, :2 * M_HEADS].set(w[:, o_if:o_rest]).astype(BF16)
        p, gif = _inproj_call(h, w_main, w_if)
        gif_t = gif[:, :2 * M_HEADS].T
        y_m = _mlstm_call(p, gif, gif_t, conv_w[l], conv_b[l], wq_m[l], wk_m[l], wv_m[l],
                          b_i[l], b_f[l], m_norm_g[l])
        lam_init = 0.8 - 0.6 * math.exp(-0.3 * l)
        y_a = _attn_call(p, lam_q1[l], lam_k1[l], lam_q2[l], lam_k2[l], a_norm_g[l], lam_init)
        h1, comb = _merge_call(y_m, y_a, p, h, w_branch_m[l].astype(BF16),
                               w_branch_a[l].astype(BF16), w_out[l].astype(BF16),
                               ln1_g[l], ln1_b[l], w_router, router_bias, alpha)
        h = _moe_call(h1, comb, w_gate_e[l].astype(BF16), w_up_e[l].astype(BF16),
                      w_down_e[l].astype(BF16), ln2_g[l], ln2_b[l], alpha)
    return h[LEAD:length][None].astype(x.dtype)
```

```python
import functools
import math

import jax
import jax.numpy as jnp
from jax import lax
from jax.experimental import pallas as pl
from jax.experimental.pallas import tpu as pltpu

F32 = jnp.float32
BF16 = jnp.bfloat16

D_MODEL = 1024
N_META = 16
LEAD = 128
N_PAD = LEAD - N_META
M_HEADS = 4
M_HEAD_DIM = D_MODEL // M_HEADS
CONV_W = 4
A_HEADS = 8
A_HEAD_DIM = D_MODEL // (2 * A_HEADS)
A_VDIM = 2 * A_HEAD_DIM
N_EXPERTS = 16
N_GROUPS = 4
EXPERTS_PER_GROUP = N_EXPERTS // N_GROUPS
LN_EPS = 1e-5
NEG = -1e30

LANES = 128
BF16_SUBLANES = 16
VMEM_LIMIT_BYTES = 56 * 1024 * 1024

ROW_ALIGN = 1536
LN_TILE = 512
PROJ_TILE_M = 1536
PROJ_TILE_N = 1024
M_CHUNK = 256
ATTN_TQ = 1536
ATTN_TK = 768
ATTN_LANE_BLOCK = 256
MERGE_TILE = 512
MOE_TILE = 1536
MOE_SLOTS = 256

COL_XM, COL_OPRE, COL_K, COL_GA, COL_GB = range(5)
N_ROW_BLOCKS = 5


def _params(semantics):
    return pltpu.CompilerParams(dimension_semantics=semantics,
                                vmem_limit_bytes=VMEM_LIMIT_BYTES)


def _layer_norm(x, g, b):
    mu = jnp.mean(x, axis=-1, keepdims=True)
    xc = x - mu
    var = jnp.mean(xc * xc, axis=-1, keepdims=True)
    return xc * lax.rsqrt(var + LN_EPS) * g + b


def _sigmoid(x):
    return 1.0 / (1.0 + jnp.exp(-x))


def _ln_kernel(x_ref, g_ref, b_ref, o_ref):
    o_ref[...] = _layer_norm(x_ref[...], g_ref[...], b_ref[...])


def _ln_call(x, g, b):
    n = x.shape[0]
    row = pl.BlockSpec((1, D_MODEL), lambda i: (0, 0))
    return pl.pallas_call(
        _ln_kernel,
        out_shape=jax.ShapeDtypeStruct((n, D_MODEL), F32),
        grid=(n // LN_TILE,),
        in_specs=[pl.BlockSpec((LN_TILE, D_MODEL), lambda i: (i, 0)), row, row],
        out_specs=pl.BlockSpec((LN_TILE, D_MODEL), lambda i: (i, 0)),
        compiler_params=_params(("parallel",)),
        name="ln_in",
    )(x, g.reshape(1, -1), b.reshape(1, -1))


def _inproj_kernel(h_ref, w_ref, wif_ref, p_ref, gif_ref, qt_ref, vt_ref):
    j = pl.program_id(1)
    tk = ATTN_TK
    hb = h_ref[...].astype(BF16)

    def project():
        return jnp.dot(hb, w_ref[...], preferred_element_type=F32)

    @pl.when(j < N_ROW_BLOCKS)
    def _():
        p_ref[...] = project().astype(BF16)

    def put_transposed(o_ref):
        res = project()
        for r in range(PROJ_TILE_M // tk):
            o_ref[r] = res[r * tk:(r + 1) * tk, :].T.astype(BF16)

    @pl.when(j == N_ROW_BLOCKS)
    def _():
        put_transposed(qt_ref)

    @pl.when(j == N_ROW_BLOCKS + 1)
    def _():
        put_transposed(vt_ref)

    @pl.when(j == 0)
    def _():
        gif_ref[...] = jnp.dot(hb, wif_ref[...], preferred_element_type=F32)


def _inproj_call(h, w_main, w_if):
    n = h.shape[0]
    tm, tk = PROJ_TILE_M, ATTN_TK
    t_shape = jax.ShapeDtypeStruct((n // tk, D_MODEL, tk), BF16)
    t_spec = pl.BlockSpec((tm // tk, D_MODEL, tk), lambda i, j: (i, 0, 0))
    return pl.pallas_call(
        _inproj_kernel,
        out_shape=(jax.ShapeDtypeStruct((n, N_ROW_BLOCKS * D_MODEL), BF16),
                   jax.ShapeDtypeStruct((n, LANES), F32), t_shape, t_shape),
        grid=(n // tm, w_main.shape[1] // PROJ_TILE_N),
        in_specs=[pl.BlockSpec((tm, D_MODEL), lambda i, j: (i, 0)),
                  pl.BlockSpec((D_MODEL, PROJ_TILE_N), lambda i, j: (0, j)),
                  pl.BlockSpec((D_MODEL, LANES), lambda i, j: (0, 0))],
        out_specs=(pl.BlockSpec((tm, PROJ_TILE_N),
                                lambda i, j: (i, jnp.minimum(j, N_ROW_BLOCKS - 1))),
                   pl.BlockSpec((tm, LANES), lambda i, j: (i, 0)), t_spec, t_spec),
        compiler_params=_params(("parallel", "arbitrary")),
        name="in_proj",
    )(h, w_main, w_if)


def _log_sigmoid(x):
    return jnp.minimum(x, 0.0) - jnp.log1p(jnp.exp(-jnp.abs(x)))


def _mlstm_kernel(xm_ref, halo_ref, opre_ref, gcol_ref, grow_ref, convw_ref, convb_ref,
                  wq_ref, wk_ref, wv_ref, bcol_ref, brow_ref, ng_ref,
                  y_ref, xs_ref, c_ref, n_ref, m_ref):
    c = pl.program_id(0)
    t = M_CHUNK
    halo = BF16_SUBLANES

    @pl.when(c == 0)
    def _():
        c_ref[...] = jnp.zeros_like(c_ref)
        n_ref[...] = jnp.zeros_like(n_ref)
        m_ref[...] = jnp.zeros_like(m_ref)

    pos_col = c * t + lax.broadcasted_iota(jnp.int32, (t, 1), 0)
    valid_col = pos_col >= N_PAD
    xm = jnp.where(valid_col, xm_ref[...].astype(F32), 0.0)
    halo_pos = c * t - halo + lax.broadcasted_iota(jnp.int32, (halo, 1), 0)
    xs_ref[0:halo, :] = jnp.where(halo_pos >= N_PAD, halo_ref[...].astype(F32), 0.0)
    xs_ref[halo:halo + t, :] = xm
    conv = xm * convw_ref[CONV_W - 1:CONV_W, :] + convb_ref[...]
    for k in range(CONV_W - 1):
        off = halo - (CONV_W - 1) + k
        conv = conv + xs_ref[off:off + t, :] * convw_ref[k:k + 1, :]
    xc = conv * _sigmoid(conv)

    gc = gcol_ref[...] + bcol_ref[...]
    lane = lax.broadcasted_iota(jnp.int32, gc.shape, 1)
    lg_col = jnp.where(lane < M_HEADS,
                       jnp.where(valid_col, gc, NEG),
                       jnp.where(valid_col, _log_sigmoid(gc), 0.0))
    gr = grow_ref[...] + brow_ref[:, 0:1]
    valid_row = (c * t + lax.broadcasted_iota(jnp.int32, (1, t), 1)) >= N_PAD
    sub = lax.broadcasted_iota(jnp.int32, gr.shape, 0)
    lg_row = jnp.where(sub < M_HEADS,
                       jnp.where(valid_row, gr, NEG),
                       jnp.where(valid_row, _log_sigmoid(gr), 0.0))
    r_idx = lax.broadcasted_iota(jnp.int32, (t, t), 0)
    c_idx = lax.broadcasted_iota(jnp.int32, (t, t), 1)
    causal = c_idx <= r_idx
    tril = jnp.where(causal, 1.0, 0.0).astype(F32)
    triu = jnp.where(r_idx <= c_idx, 1.0, 0.0).astype(F32)
    cum_col = jnp.dot(tril, lg_col, preferred_element_type=F32, precision=lax.Precision.HIGHEST)
    cum_row = jnp.dot(lg_row, triu, preferred_element_type=F32, precision=lax.Precision.HIGHEST)

    nt_dims = (((1,), (1,)), ((), ()))
    tn_dims = (((0,), (0,)), ((), ()))
    for h in range(M_HEADS):
        sl = slice(h * M_HEAD_DIM, (h + 1) * M_HEAD_DIM)
        xch = xc[:, sl].astype(BF16)
        q = jnp.dot(xch, wq_ref[h], preferred_element_type=F32)
        k = jnp.dot(xch, wk_ref[h], preferred_element_type=F32) * (M_HEAD_DIM ** -0.5)
        v = jnp.dot(xm[:, sl].astype(BF16), wv_ref[h], preferred_element_type=F32)
        qb = q.astype(BF16)
        kb = k.astype(BF16)
        vb = v.astype(BF16)

        li_c = lg_col[:, h:h + 1]
        b_c = cum_col[:, M_HEADS + h:M_HEADS + h + 1]
        li_r = lg_row[h:h + 1, :]
        b_r = cum_row[M_HEADS + h:M_HEADS + h + 1, :]
        g = b_r[:, t - 1:t]
        m_prev = m_ref[h][:, 0:1]
        c_prev = c_ref[h]
        n_prev = n_ref[h]

        d_mat = jnp.where(causal, b_c - b_r + li_r, NEG)
        inter = b_c + m_prev
        m_t = jnp.maximum(inter, jnp.max(d_mat, axis=-1, keepdims=True))
        s = lax.dot_general(qb, kb, nt_dims, preferred_element_type=F32)
        w_intra = s * jnp.exp(d_mat - m_t)
        scale_inter = jnp.exp(inter - m_t)
        num = (jnp.dot(w_intra.astype(BF16), vb, preferred_element_type=F32)
               + scale_inter * jnp.dot(qb, c_prev.astype(BF16), preferred_element_type=F32))
        den = (jnp.sum(w_intra, axis=-1, keepdims=True)
               + scale_inter * jnp.sum(q * n_prev, axis=-1, keepdims=True))
        hh = num / jnp.maximum(jnp.abs(den), jnp.exp(-m_t))

        a = g - b_c + li_c
        m_new = jnp.maximum(g + m_prev, jnp.max(a, axis=0, keepdims=True))
        w_state = jnp.exp(a - m_new)
        decay = jnp.exp(g + m_prev - m_new)
        kw = k * w_state
        c_ref[h] = decay * c_prev + lax.dot_general(kw.astype(BF16), vb, tn_dims,
                                                    preferred_element_type=F32)
        n_ref[h] = decay * n_prev + jnp.sum(kw, axis=0, keepdims=True)
        m_ref[h] = jnp.broadcast_to(m_new, (1, LANES))

        hh = _sigmoid(opre_ref[:, sl].astype(F32)) * hh
        mu = jnp.mean(hh, axis=-1, keepdims=True)
        hc = hh - mu
        var = jnp.mean(hc * hc, axis=-1, keepdims=True)
        y_ref[:, sl] = (hc * lax.rsqrt(var + LN_EPS) * ng_ref[:, sl]).astype(BF16)


def _mlstm_call(p, gif, gif_t, conv_w, conv_b, wq, wk, wv, b_i, b_f, norm_g):
    n = p.shape[0]
    t = M_CHUNK
    halo = BF16_SUBLANES
    bias = jnp.concatenate([b_i, b_f]).astype(F32)
    bias_col = jnp.zeros((1, LANES), F32).at[0, :2 * M_HEADS].set(bias)
    bias_row = jnp.broadcast_to(bias[:, None], (2 * M_HEADS, LANES))
    const = lambda *shape: pl.BlockSpec(shape, lambda c: (0,) * len(shape))
    return pl.pallas_call(
        _mlstm_kernel,
        out_shape=jax.ShapeDtypeStruct((n, D_MODEL), BF16),
        grid=(n // t,),
        in_specs=[
            pl.BlockSpec((t, D_MODEL), lambda c: (c, COL_XM)),
            pl.BlockSpec((halo, D_MODEL),
                         lambda c: (jnp.maximum(c * (t // halo) - 1, 0), COL_XM)),
            pl.BlockSpec((t, D_MODEL), lambda c: (c, COL_OPRE)),
            pl.BlockSpec((t, LANES), lambda c: (c, 0)),
            pl.BlockSpec((2 * M_HEADS, t), lambda c: (0, c)),
            const(CONV_W, D_MODEL),
            const(1, D_MODEL),
            const(M_HEADS, M_HEAD_DIM, M_HEAD_DIM),
            const(M_HEADS, M_HEAD_DIM, M_HEAD_DIM),
            const(M_HEADS, M_HEAD_DIM, M_HEAD_DIM),
            const(1, LANES),
            const(2 * M_HEADS, LANES),
            const(1, D_MODEL),
        ],
        out_specs=pl.BlockSpec((t, D_MODEL), lambda c: (c, 0)),
        scratch_shapes=[
            pltpu.VMEM((t + halo, D_MODEL), F32),
            pltpu.VMEM((M_HEADS, M_HEAD_DIM, M_HEAD_DIM), F32),
            pltpu.VMEM((M_HEADS, 1, M_HEAD_DIM), F32),
            pltpu.VMEM((M_HEADS, 1, LANES), F32),
        ],
        compiler_params=_params(("arbitrary",)),
        name="mlstm",
    )(p, p, p, gif, gif_t, conv_w, conv_b.reshape(1, -1),
      wq.astype(BF16), wk.astype(BF16), wv.astype(BF16), bias_col, bias_row,
      norm_g.reshape(1, -1))


def _attn_kernel(qt_ref, k_ref, vt_ref, lq1_ref, lk1_ref, lq2_ref, lk2_ref, g_ref,
                 o_ref, qz_ref, s0_ref, s1_ref, mt0_ref, mt1_ref, m_ref, acc_ref,
                 *, lam_init):
    i = pl.program_id(1)
    tq, tk = ATTN_TQ, ATTN_TK
    s_refs = (s0_ref, s1_ref)
    mt_refs = (mt0_ref, mt1_ref)

    for r in range(tq // tk):
        qt = qt_ref[r]
        sub = lax.broadcasted_iota(jnp.int32, qt.shape, 0)
        zero = jnp.zeros_like(qt)
        qz_ref[:, r * tk:(r + 1) * tk] = jnp.where(sub < A_HEAD_DIM, qt, zero)
        qz_ref[:, tq + r * tk:tq + (r + 1) * tk] = jnp.where(sub >= A_HEAD_DIM, qt, zero)
    m_ref[...] = jnp.full_like(m_ref, NEG)
    acc_ref[...] = jnp.zeros_like(acc_ref)

    n_blocks = 2 * tq // ATTN_LANE_BLOCK
    lane_blocks = [slice(b * ATTN_LANE_BLOCK, (b + 1) * ATTN_LANE_BLOCK) for b in range(n_blocks)]

    def scores_block(j, buf, masked, b):
        lanes = lane_blocks[b]
        start = pl.multiple_of(j * tk, tk)
        kt = k_ref[pl.ds(start, tk), :]
        s = jnp.dot(kt, qz_ref[:, lanes], preferred_element_type=F32)
        if masked:
            kpos = j * tk + lax.broadcasted_iota(jnp.int32, s.shape, 0)
            col = b * ATTN_LANE_BLOCK + lax.broadcasted_iota(jnp.int32, s.shape, 1)
            qpos = i * tq + jnp.where(col >= tq, col - tq, col)
            s = jnp.where(kpos <= qpos, jnp.where(kpos >= N_PAD, s, NEG), NEG)
        s_refs[buf][:, lanes] = s
        mt_refs[buf][:, lanes] = jnp.max(s, axis=0, keepdims=True)

    ones_rows = jnp.ones((BF16_SUBLANES, tk), BF16)

    def accumulate_block(j, buf, b):
        lanes = lane_blocks[b]
        m_prev = m_ref[:, lanes]
        m_new = jnp.maximum(m_prev, mt_refs[buf][:, lanes])
        alpha = jnp.exp2(m_prev - m_new)
        p = jnp.exp2((s_refs[buf][:, lanes] - m_new).astype(BF16))
        m_ref[:, lanes] = m_new
        vt = jnp.concatenate([vt_ref[j], ones_rows], axis=0)
        pv = jnp.dot(vt, p, preferred_element_type=F32)
        acc_ref[:, lanes] = alpha * acc_ref[:, lanes] + pv

    def scores(j, buf, masked):
        for b in range(n_blocks):
            scores_block(j, buf, masked, b)

    def accumulate(j, buf):
        for b in range(n_blocks):
            accumulate_block(j, buf, b)

    def stage(j, masked_next):
        for buf in (0, 1):
            @pl.when(j % 2 == buf)
            def _():
                for b in range(n_blocks):
                    accumulate_block(j, buf, b)
                    scores_block(j + 1, 1 - buf, masked_next, b)

    n_diag = tq // tk
    n_k = (i + 1) * n_diag
    first_diag = n_k - n_diag
    scores(0, 0, True)

    def body(j, carry):
        stage(j, False)
        return carry

    def body_masked(j, carry):
        stage(j, True)
        return carry

    lax.fori_loop(0, first_diag - 1, body, 0)
    lax.fori_loop(jnp.maximum(first_diag - 1, 0), n_k - 1, body_masked, 0)

    for buf in (0, 1):
        @pl.when((n_k - 1) % 2 == buf)
        def _():
            accumulate(n_k - 1, buf)

    lam = (jnp.exp(jnp.sum(lq1_ref[...] * lk1_ref[...], axis=-1, keepdims=True))
           - jnp.exp(jnp.sum(lq2_ref[...] * lk2_ref[...], axis=-1, keepdims=True))
           + lam_init)
    out = acc_ref[0:A_VDIM, :] / acc_ref[A_VDIM:A_VDIM + 1, :]
    o = out[:, 0:tq] - lam * out[:, tq:2 * tq]
    o = o * lax.rsqrt(jnp.mean(o * o, axis=0, keepdims=True) + LN_EPS)
    o = o * g_ref[...] * (1.0 - lam_init)
    o_ref[...] = o.T.astype(BF16)


def _attn_call(p, q_t, v_t, lq1, lk1, lq2, lk2, norm_g, lam_init):
    n = p.shape[0]
    tq, tk = ATTN_TQ, ATTN_TK
    hpb = D_MODEL // A_VDIM
    vec = lambda d: pl.BlockSpec((1, d), lambda h, i: (0, 0))
    return pl.pallas_call(
        functools.partial(_attn_kernel, lam_init=lam_init),
        out_shape=jax.ShapeDtypeStruct((n, D_MODEL), BF16),
        grid=(A_HEADS, n // tq),
        in_specs=[
            pl.BlockSpec((tq // tk, A_VDIM, tk), lambda h, i: (i, h, 0)),
            pl.BlockSpec((n, A_VDIM), lambda h, i: (0, COL_K * hpb + h)),
            pl.BlockSpec((n // tk, A_VDIM, tk), lambda h, i: (0, h, 0)),
            vec(A_HEAD_DIM), vec(A_HEAD_DIM), vec(A_HEAD_DIM), vec(A_HEAD_DIM),
            pl.BlockSpec((A_VDIM, 1), lambda h, i: (0, 0)),
        ],
        out_specs=pl.BlockSpec((tq, A_VDIM), lambda h, i: (i, h)),
        scratch_shapes=[
            pltpu.VMEM((A_VDIM, 2 * tq), BF16),
            pltpu.VMEM((tk, 2 * tq), F32),
            pltpu.VMEM((tk, 2 * tq), F32),
            pltpu.VMEM((1, 2 * tq), F32),
            pltpu.VMEM((1, 2 * tq), F32),
            pltpu.VMEM((1, 2 * tq), F32),
            pltpu.VMEM((A_VDIM + BF16_SUBLANES, 2 * tq), F32),
        ],
        compiler_params=_params(("parallel", "arbitrary")),
        name="diff_attn",
    )(q_t, p, v_t, lq1.reshape(1, -1), lk1.reshape(1, -1), lq2.reshape(1, -1),
      lk2.reshape(1, -1), norm_g.reshape(-1, 1))


def _merge_kernel(ym_ref, ya_ref, ga_ref, gb_ref, h_ref, wm_ref, wa_ref, wo_ref,
                  g_ref, b_ref, wr_ref, rb_ref, h1_ref, comb_ref, *, alpha):
    bm = jnp.dot(ym_ref[...], wm_ref[...], preferred_element_type=F32)
    ba = jnp.dot(ya_ref[...], wa_ref[...], preferred_element_type=F32)
    merged = (_sigmoid(ga_ref[...].astype(F32)) * bm
              + _sigmoid(gb_ref[...].astype(F32)) * ba)
    proj = jnp.dot(merged.astype(BF16), wo_ref[...], preferred_element_type=F32)
    h1 = _layer_norm(alpha * h_ref[...] + proj, g_ref[...], b_ref[...])
    h1_ref[...] = h1

    logits = jnp.dot(h1, wr_ref[...], preferred_element_type=F32,
                     precision=lax.Precision.HIGHEST)
    lane = lax.broadcasted_iota(jnp.int32, logits.shape, 1)
    is_expert = lane < N_EXPERTS
    logits = jnp.where(is_expert, logits, NEG)
    ex = jnp.exp(logits - jnp.max(logits, axis=-1, keepdims=True))
    probs = ex / jnp.sum(ex, axis=-1, keepdims=True)
    sel = jnp.where(is_expert, probs + rb_ref[...], NEG)
    group = lane // EXPERTS_PER_GROUP
    best = jnp.max(jnp.where(group == 0, sel, NEG), axis=-1, keepdims=True)
    top_group = jnp.zeros_like(best, dtype=jnp.int32)
    for gidx in range(1, N_GROUPS):
        score = jnp.max(jnp.where(group == gidx, sel, NEG), axis=-1, keepdims=True)
        top_group = jnp.where(score > best, gidx, top_group)
        best = jnp.maximum(best, score)
    cand = jnp.where(group == top_group, sel, NEG)
    v1 = jnp.max(cand, axis=-1, keepdims=True)
    i1 = jnp.min(jnp.where(cand == v1, lane, LANES), axis=-1, keepdims=True)
    cand2 = jnp.where(lane == i1, NEG, cand)
    v2 = jnp.max(cand2, axis=-1, keepdims=True)
    i2 = jnp.min(jnp.where(cand2 == v2, lane, LANES), axis=-1, keepdims=True)
    p1 = jnp.sum(jnp.where(lane == i1, probs, 0.0), axis=-1, keepdims=True)
    p2 = jnp.sum(jnp.where(lane == i2, probs, 0.0), axis=-1, keepdims=True)
    tot = p1 + p2
    comb_ref[...] = (jnp.where(lane == i1, p1 / tot, 0.0)
                     + jnp.where(lane == i2, p2 / tot, 0.0))


def _merge_call(y_m, y_a, p, h, w_m, w_a, w_o, ln_g, ln_b, w_router, router_bias, alpha):
    n = h.shape[0]
    tm = MERGE_TILE
    full = pl.BlockSpec((D_MODEL, D_MODEL), lambda i: (0, 0))
    row = pl.BlockSpec((1, D_MODEL), lambda i: (0, 0))
    tile = lambda col: pl.BlockSpec((tm, D_MODEL), lambda i: (i, col))
    wr = jnp.zeros((D_MODEL, LANES), F32).at[:, :N_EXPERTS].set(w_router)
    rb = jnp.zeros((1, LANES), F32).at[0, :N_EXPERTS].set(router_bias)
    return pl.pallas_call(
        functools.partial(_merge_kernel, alpha=alpha),
        out_shape=(jax.ShapeDtypeStruct((n, D_MODEL), F32),
                   jax.ShapeDtypeStruct((n, LANES), F32)),
        grid=(n // tm,),
        in_specs=[tile(0), tile(0), tile(COL_GA), tile(COL_GB), tile(0),
                  full, full, full, row, row,
                  pl.BlockSpec((D_MODEL, LANES), lambda i: (0, 0)),
                  pl.BlockSpec((1, LANES), lambda i: (0, 0))],
        out_specs=(tile(0), pl.BlockSpec((tm, LANES), lambda i: (i, 0))),
        compiler_params=_params(("parallel",)),
        name="merge_router",
    )(y_m, y_a, p, p, h, w_m, w_a, w_o, ln_g.reshape(1, -1), ln_b.reshape(1, -1), wr, rb)


def _moe_kernel(h_ref, comb_ref, combt_ref, wg_ref, wu_ref, wd_ref, g_ref, b_ref, o_ref,
                xb_ref, acc_ref, rankc_ref, rankr_ref, whl_ref, *, alpha):
    e = pl.program_id(1)
    tm = MOE_TILE
    ns = MOE_SLOTS

    @pl.when(e == 0)
    def _():
        xb_ref[...] = h_ref[...].astype(BF16)
        acc_ref[...] = jnp.zeros_like(acc_ref)
        comb = comb_ref[...]
        r_idx = lax.broadcasted_iota(jnp.int32, (tm, tm), 0)
        c_idx = lax.broadcasted_iota(jnp.int32, (tm, tm), 1)
        before = jnp.where(c_idx < r_idx, 1.0, 0.0).astype(BF16)
        rankc_ref[...] = jnp.dot(before, jnp.where(comb > 0.0, 1.0, 0.0).astype(BF16),
                                 preferred_element_type=F32)
        after = jnp.where(r_idx < c_idx, 1.0, 0.0).astype(BF16)
        rankr_ref[...] = jnp.dot(jnp.where(combt_ref[...] > 0.0, 1.0, 0.0).astype(BF16), after,
                                 preferred_element_type=F32)
        hi = comb.astype(BF16).astype(F32)
        lo = comb - hi
        whl_ref[...] = (hi + pltpu.roll(lo, N_EXPERTS, 1)).astype(BF16)

    comb = comb_ref[...]
    lane = lax.broadcasted_iota(jnp.int32, comb.shape, 1)
    mine = lane == e
    routed_c = jnp.sum(jnp.where(mine, comb, 0.0), axis=-1, keepdims=True) > 0.0
    rank_c = jnp.sum(jnp.where(mine, rankc_ref[...], 0.0), axis=-1, keepdims=True)
    routed_r = combt_ref[pl.ds(e, 1), :] > 0.0
    rank_r = rankr_ref[pl.ds(e, 1), :]
    count = jnp.sum(jnp.where(routed_r, 1.0, 0.0)).astype(jnp.int32)
    n_blocks = (count + ns - 1) // ns

    def block(b, carry):
        base = (b * ns).astype(F32)
        slot_r = lax.broadcasted_iota(jnp.int32, (ns, tm), 0).astype(F32) + base
        sel = jnp.where(routed_r & (rank_r == slot_r), 1.0, 0.0).astype(BF16)
        slot_c = lax.broadcasted_iota(jnp.int32, (tm, ns), 1).astype(F32) + base
        sel_t = jnp.where(routed_c & (rank_c == slot_c), 1.0, 0.0).astype(BF16)
        xs = jnp.dot(sel, xb_ref[...], preferred_element_type=F32).astype(BF16)
        ws = jnp.dot(sel, whl_ref[...], preferred_element_type=F32)
        lane_s = lax.broadcasted_iota(jnp.int32, ws.shape, 1)
        w_slot = jnp.sum(jnp.where((lane_s == e) | (lane_s == e + N_EXPERTS), ws, 0.0),
                         axis=-1, keepdims=True)
        gate = jnp.dot(xs, wg_ref[0], preferred_element_type=F32)
        up = jnp.dot(xs, wu_ref[0], preferred_element_type=F32)
        mid = (gate * _sigmoid(gate) * up).astype(BF16)
        out = jnp.dot(mid, wd_ref[0], preferred_element_type=F32) * w_slot
        acc_ref[...] += jnp.dot(sel_t, out.astype(BF16), preferred_element_type=F32)
        return carry

    lax.fori_loop(0, n_blocks, block, 0)

    @pl.when(e == N_EXPERTS - 1)
    def _():
        o_ref[...] = _layer_norm(alpha * h_ref[...] + acc_ref[...], g_ref[...], b_ref[...])


def _moe_call(h1, comb, w_gate, w_up, w_down, ln_g, ln_b, alpha):
    n = h1.shape[0]
    tm = MOE_TILE
    comb_t = comb[:, :N_EXPERTS].T
    wspec = pl.BlockSpec((1, D_MODEL, D_MODEL), lambda i, e: (e, 0, 0))
    row = pl.BlockSpec((1, D_MODEL), lambda i, e: (0, 0))
    return pl.pallas_call(
        functools.partial(_moe_kernel, alpha=alpha),
        out_shape=jax.ShapeDtypeStruct((n, D_MODEL), F32),
        grid=(n // tm, N_EXPERTS),
        in_specs=[pl.BlockSpec((tm, D_MODEL), lambda i, e: (i, 0),
                               pipeline_mode=pl.Buffered(1)),
                  pl.BlockSpec((tm, LANES), lambda i, e: (i, 0)),
                  pl.BlockSpec((N_EXPERTS, tm), lambda i, e: (0, i)),
                  wspec, wspec, wspec, row, row],
        out_specs=pl.BlockSpec((tm, D_MODEL), lambda i, e: (i, 0)),
        scratch_shapes=[pltpu.VMEM((tm, D_MODEL), BF16),
                        pltpu.VMEM((tm, D_MODEL), F32),
                        pltpu.VMEM((tm, LANES), F32),
                        pltpu.VMEM((N_EXPERTS, tm), F32),
                        pltpu.VMEM((tm, LANES), BF16)],
        compiler_params=_params(("parallel", "arbitrary")),
        name="moe",
    )(h1, comb, comb_t, w_gate, w_up, w_down, ln_g.reshape(1, -1), ln_b.reshape(1, -1))


def kernel(x, meta_tokens, ln_in_g, ln_in_b, w_in, conv_w, conv_b, wq_m, wk_m, wv_m, b_i, b_f, m_norm_g, lam_q1, lam_k1, lam_q2, lam_k2, a_norm_g, w_branch_m, w_branch_a, w_out, ln1_g, ln1_b, ln2_g, ln2_b, w_router, router_bias, w_gate_e, w_up_e, w_down_e):
    bsz, seq, dm = x.shape
    assert bsz == 1 and dm == D_MODEL
    depth = w_in.shape[0]
    alpha = float((2 * depth) ** 0.25)
    length = LEAD + seq
    n = -(-length // ROW_ALIGN) * ROW_ALIGN

    rows = jnp.concatenate([
        jnp.zeros((N_PAD, dm), F32), meta_tokens.astype(F32), x[0].astype(F32),
        jnp.zeros((n - length, dm), F32)], axis=0)
    h = _ln_call(rows, ln_in_g, ln_in_b)

    o_if = 2 * D_MODEL
    o_rest = o_if + 2 * M_HEADS
    for l in range(depth):
        w = w_in[l]
        q_scale = A_HEAD_DIM ** -0.5 * math.log2(math.e)
        blk = lambda b: w[:, o_rest + b * D_MODEL:o_rest + (b + 1) * D_MODEL]
        w_q, w_k, w_v, w_ga, w_gb = (blk(b) for b in range(5))
        w_main = jnp.concatenate([w[:, :o_if], w_k, w_ga, w_gb, w_q * q_scale, w_v],
                                 axis=1).astype(BF16)
        w_if = jnp.zeros((dm, LANES), F32).at[:, :2 * M_HEADS].set(w[:, o_if:o_rest]).astype(BF16)
        p, gif, q_t, v_t = _inproj_call(h, w_main, w_if)
        gif_t = gif[:, :2 * M_HEADS].T
        y_m = _mlstm_call(p, gif, gif_t, conv_w[l], conv_b[l], wq_m[l], wk_m[l], wv_m[l],
                          b_i[l], b_f[l], m_norm_g[l])
        lam_init = 0.8 - 0.6 * math.exp(-0.3 * l)
        y_a = _attn_call(p, q_t, v_t, lam_q1[l], lam_k1[l], lam_q2[l], lam_k2[l], a_norm_g[l],
                         lam_init)
        h1, comb = _merge_call(y_m, y_a, p, h, w_branch_m[l].astype(BF16),
                               w_branch_a[l].astype(BF16), w_out[l].astype(BF16),
                               ln1_g[l], ln1_b[l], w_router, router_bias, alpha)
        h = _moe_call(h1, comb, w_gate_e[l].astype(BF16), w_up_e[l].astype(BF16),
                      w_down_e[l].astype(BF16), ln2_g[l], ln2_b[l], alpha)
    return h[LEAD:length][None].astype(x.dtype)
```

```python
import functools
import math

import jax
import jax.numpy as jnp
from jax import lax
from jax.experimental import pallas as pl
from jax.experimental.pallas import tpu as pltpu

F32 = jnp.float32
BF16 = jnp.bfloat16

D_MODEL = 1024
N_META = 16
LEAD = 128
N_PAD = LEAD - N_META
M_HEADS = 4
M_HEAD_DIM = D_MODEL // M_HEADS
CONV_W = 4
A_HEADS = 8
A_HEAD_DIM = D_MODEL // (2 * A_HEADS)
A_VDIM = 2 * A_HEAD_DIM
N_EXPERTS = 16
N_GROUPS = 4
EXPERTS_PER_GROUP = N_EXPERTS // N_GROUPS
LN_EPS = 1e-5
NEG = -1e30

LANES = 128
BF16_SUBLANES = 16
VMEM_LIMIT_BYTES = 56 * 1024 * 1024

ROW_ALIGN = 1536
LN_TILE = 512
PROJ_TILE_M = 1536
PROJ_TILE_N = 1024
M_CHUNK = 256
ATTN_TQ = 1536
ATTN_TK = 768
ATTN_LANE_BLOCK = 256
MERGE_TILE = 512
MOE_TILE = 1536
MOE_SLOTS = 256

COL_XM, COL_OPRE, COL_K, COL_GA, COL_GB = range(5)
N_ROW_BLOCKS = 5


def _params(semantics):
    return pltpu.CompilerParams(dimension_semantics=semantics,
                                vmem_limit_bytes=VMEM_LIMIT_BYTES)


def _layer_norm(x, g, b):
    mu = jnp.mean(x, axis=-1, keepdims=True)
    xc = x - mu
    var = jnp.mean(xc * xc, axis=-1, keepdims=True)
    return xc * lax.rsqrt(var + LN_EPS) * g + b


def _sigmoid(x):
    return 1.0 / (1.0 + jnp.exp(-x))


def _ln_kernel(x_ref, g_ref, b_ref, o_ref):
    o_ref[...] = _layer_norm(x_ref[...], g_ref[...], b_ref[...])


def _ln_call(x, g, b):
    n = x.shape[0]
    row = pl.BlockSpec((1, D_MODEL), lambda i: (0, 0))
    return pl.pallas_call(
        _ln_kernel,
        out_shape=jax.ShapeDtypeStruct((n, D_MODEL), F32),
        grid=(n // LN_TILE,),
        in_specs=[pl.BlockSpec((LN_TILE, D_MODEL), lambda i: (i, 0)), row, row],
        out_specs=pl.BlockSpec((LN_TILE, D_MODEL), lambda i: (i, 0)),
        compiler_params=_params(("parallel",)),
        name="ln_in",
    )(x, g.reshape(1, -1), b.reshape(1, -1))


def _inproj_kernel(h_ref, w_ref, wif_ref, p_ref, gif_ref, qt_ref, vt_ref):
    j = pl.program_id(1)
    tk = ATTN_TK
    hb = h_ref[...].astype(BF16)

    def project():
        return jnp.dot(hb, w_ref[...], preferred_element_type=F32)

    @pl.when(j < N_ROW_BLOCKS)
    def _():
        p_ref[...] = project().astype(BF16)

    def put_transposed(o_ref):
        res = project()
        for r in range(PROJ_TILE_M // tk):
            o_ref[r] = res[r * tk:(r + 1) * tk, :].T.astype(BF16)

    @pl.when(j == N_ROW_BLOCKS)
    def _():
        put_transposed(qt_ref)

    @pl.when(j == N_ROW_BLOCKS + 1)
    def _():
        put_transposed(vt_ref)

    @pl.when(j == 0)
    def _():
        gif_ref[...] = jnp.dot(hb, wif_ref[...], preferred_element_type=F32)


def _inproj_call(h, w_main, w_if):
    n = h.shape[0]
    tm, tk = PROJ_TILE_M, ATTN_TK
    t_shape = jax.ShapeDtypeStruct((n // tk, D_MODEL, tk), BF16)
    t_spec = pl.BlockSpec((tm // tk, D_MODEL, tk), lambda i, j: (i, 0, 0))
    return pl.pallas_call(
        _inproj_kernel,
        out_shape=(jax.ShapeDtypeStruct((n, N_ROW_BLOCKS * D_MODEL), BF16),
                   jax.ShapeDtypeStruct((n, LANES), F32), t_shape, t_shape),
        grid=(n // tm, w_main.shape[1] // PROJ_TILE_N),
        in_specs=[pl.BlockSpec((tm, D_MODEL), lambda i, j: (i, 0)),
                  pl.BlockSpec((D_MODEL, PROJ_TILE_N), lambda i, j: (0, j)),
                  pl.BlockSpec((D_MODEL, LANES), lambda i, j: (0, 0))],
        out_specs=(pl.BlockSpec((tm, PROJ_TILE_N),
                                lambda i, j: (i, jnp.minimum(j, N_ROW_BLOCKS - 1))),
                   pl.BlockSpec((tm, LANES), lambda i, j: (i, 0)), t_spec, t_spec),
        compiler_params=_params(("parallel", "arbitrary")),
        name="in_proj",
    )(h, w_main, w_if)


def _log_sigmoid(x):
    return jnp.minimum(x, 0.0) - jnp.log1p(jnp.exp(-jnp.abs(x)))


def _mlstm_kernel(xm_ref, halo_ref, opre_ref, gcol_ref, grow_ref, convw_ref, convb_ref,
                  wq_ref, wk_ref, wv_ref, bcol_ref, brow_ref, ng_ref,
                  y_ref, xs_ref, c_ref, n_ref, m_ref):
    c = pl.program_id(0)
    t = M_CHUNK
    halo = BF16_SUBLANES

    @pl.when(c == 0)
    def _():
        c_ref[...] = jnp.zeros_like(c_ref)
        n_ref[...] = jnp.zeros_like(n_ref)
        m_ref[...] = jnp.zeros_like(m_ref)

    pos_col = c * t + lax.broadcasted_iota(jnp.int32, (t, 1), 0)
    valid_col = pos_col >= N_PAD
    xm = jnp.where(valid_col, xm_ref[...].astype(F32), 0.0)
    halo_pos = c * t - halo + lax.broadcasted_iota(jnp.int32, (halo, 1), 0)
    xs_ref[0:halo, :] = jnp.where(halo_pos >= N_PAD, halo_ref[...].astype(F32), 0.0)
    xs_ref[halo:halo + t, :] = xm
    conv = xm * convw_ref[CONV_W - 1:CONV_W, :] + convb_ref[...]
    for k in range(CONV_W - 1):
        off = halo - (CONV_W - 1) + k
        conv = conv + xs_ref[off:off + t, :] * convw_ref[k:k + 1, :]
    xc = conv * _sigmoid(conv)

    gc = gcol_ref[...] + bcol_ref[...]
    lane = lax.broadcasted_iota(jnp.int32, gc.shape, 1)
    lg_col = jnp.where(lane < M_HEADS,
                       jnp.where(valid_col, gc, NEG),
                       jnp.where(valid_col, _log_sigmoid(gc), 0.0))
    gr = grow_ref[...] + brow_ref[:, 0:1]
    valid_row = (c * t + lax.broadcasted_iota(jnp.int32, (1, t), 1)) >= N_PAD
    sub = lax.broadcasted_iota(jnp.int32, gr.shape, 0)
    lg_row = jnp.where(sub < M_HEADS,
                       jnp.where(valid_row, gr, NEG),
                       jnp.where(valid_row, _log_sigmoid(gr), 0.0))
    r_idx = lax.broadcasted_iota(jnp.int32, (t, t), 0)
    c_idx = lax.broadcasted_iota(jnp.int32, (t, t), 1)
    causal = c_idx <= r_idx
    tril = jnp.where(causal, 1.0, 0.0).astype(F32)
    triu = jnp.where(r_idx <= c_idx, 1.0, 0.0).astype(F32)
    cum_col = jnp.dot(tril, lg_col, preferred_element_type=F32, precision=lax.Precision.HIGHEST)
    cum_row = jnp.dot(lg_row, triu, preferred_element_type=F32, precision=lax.Precision.HIGHEST)

    nt_dims = (((1,), (1,)), ((), ()))
    tn_dims = (((0,), (0,)), ((), ()))
    for h in range(M_HEADS):
        sl = slice(h * M_HEAD_DIM, (h + 1) * M_HEAD_DIM)
        xch = xc[:, sl].astype(BF16)
        q = jnp.dot(xch, wq_ref[h], preferred_element_type=F32)
        k = jnp.dot(xch, wk_ref[h], preferred_element_type=F32) * (M_HEAD_DIM ** -0.5)
        v = jnp.dot(xm[:, sl].astype(BF16), wv_ref[h], preferred_element_type=F32)
        qb = q.astype(BF16)
        kb = k.astype(BF16)
        vb = v.astype(BF16)

        li_c = lg_col[:, h:h + 1]
        b_c = cum_col[:, M_HEADS + h:M_HEADS + h + 1]
        li_r = lg_row[h:h + 1, :]
        b_r = cum_row[M_HEADS + h:M_HEADS + h + 1, :]
        g = b_r[:, t - 1:t]
        m_prev = m_ref[h][:, 0:1]
        c_prev = c_ref[h]
        n_prev = n_ref[h]

        d_mat = jnp.where(causal, b_c - b_r + li_r, NEG)
        inter = b_c + m_prev
        m_t = jnp.maximum(inter, jnp.max(d_mat, axis=-1, keepdims=True))
        s = lax.dot_general(qb, kb, nt_dims, preferred_element_type=F32)
        w_intra = s * jnp.exp(d_mat - m_t)
        scale_inter = jnp.exp(inter - m_t)
        num = (jnp.dot(w_intra.astype(BF16), vb, preferred_element_type=F32)
               + scale_inter * jnp.dot(qb, c_prev.astype(BF16), preferred_element_type=F32))
        den = (jnp.sum(w_intra, axis=-1, keepdims=True)
               + scale_inter * jnp.sum(q * n_prev, axis=-1, keepdims=True))
        hh = num / jnp.maximum(jnp.abs(den), jnp.exp(-m_t))

        a = g - b_c + li_c
        m_new = jnp.maximum(g + m_prev, jnp.max(a, axis=0, keepdims=True))
        w_state = jnp.exp(a - m_new)
        decay = jnp.exp(g + m_prev - m_new)
        kw = k * w_state
        c_ref[h] = decay * c_prev + lax.dot_general(kw.astype(BF16), vb, tn_dims,
                                                    preferred_element_type=F32)
        n_ref[h] = decay * n_prev + jnp.sum(kw, axis=0, keepdims=True)
        m_ref[h] = jnp.broadcast_to(m_new, (1, LANES))

        hh = _sigmoid(opre_ref[:, sl].astype(F32)) * hh
        mu = jnp.mean(hh, axis=-1, keepdims=True)
        hc = hh - mu
        var = jnp.mean(hc * hc, axis=-1, keepdims=True)
        y_ref[:, sl] = (hc * lax.rsqrt(var + LN_EPS) * ng_ref[:, sl]).astype(BF16)


def _mlstm_call(p, gif, gif_t, conv_w, conv_b, wq, wk, wv, b_i, b_f, norm_g):
    n = p.shape[0]
    t = M_CHUNK
    halo = BF16_SUBLANES
    bias = jnp.concatenate([b_i, b_f]).astype(F32)
    bias_col = jnp.zeros((1, LANES), F32).at[0, :2 * M_HEADS].set(bias)
    bias_row = jnp.broadcast_to(bias[:, None], (2 * M_HEADS, LANES))
    const = lambda *shape: pl.BlockSpec(shape, lambda c: (0,) * len(shape))
    return pl.pallas_call(
        _mlstm_kernel,
        out_shape=jax.ShapeDtypeStruct((n, D_MODEL), BF16),
        grid=(n // t,),
        in_specs=[
            pl.BlockSpec((t, D_MODEL), lambda c: (c, COL_XM)),
            pl.BlockSpec((halo, D_MODEL),
                         lambda c: (jnp.maximum(c * (t // halo) - 1, 0), COL_XM)),
            pl.BlockSpec((t, D_MODEL), lambda c: (c, COL_OPRE)),
            pl.BlockSpec((t, LANES), lambda c: (c, 0)),
            pl.BlockSpec((2 * M_HEADS, t), lambda c: (0, c)),
            const(CONV_W, D_MODEL),
            const(1, D_MODEL),
            const(M_HEADS, M_HEAD_DIM, M_HEAD_DIM),
            const(M_HEADS, M_HEAD_DIM, M_HEAD_DIM),
            const(M_HEADS, M_HEAD_DIM, M_HEAD_DIM),
            const(1, LANES),
            const(2 * M_HEADS, LANES),
            const(1, D_MODEL),
        ],
        out_specs=pl.BlockSpec((t, D_MODEL), lambda c: (c, 0)),
        scratch_shapes=[
            pltpu.VMEM((t + halo, D_MODEL), F32),
            pltpu.VMEM((M_HEADS, M_HEAD_DIM, M_HEAD_DIM), F32),
            pltpu.VMEM((M_HEADS, 1, M_HEAD_DIM), F32),
            pltpu.VMEM((M_HEADS, 1, LANES), F32),
        ],
        compiler_params=_params(("arbitrary",)),
        name="mlstm",
    )(p, p, p, gif, gif_t, conv_w, conv_b.reshape(1, -1),
      wq.astype(BF16), wk.astype(BF16), wv.astype(BF16), bias_col, bias_row,
      norm_g.reshape(1, -1))


def _attn_kernel(qt_ref, k_ref, vt_ref, lq1_ref, lk1_ref, lq2_ref, lk2_ref, g_ref,
                 o_ref, qz_ref, s0_ref, s1_ref, mt0_ref, mt1_ref, m_ref, acc_ref,
                 *, lam_init):
    i = pl.program_id(1)
    tq, tk = ATTN_TQ, ATTN_TK
    s_refs = (s0_ref, s1_ref)
    mt_refs = (mt0_ref, mt1_ref)

    for r in range(tq // tk):
        qt = qt_ref[r]
        sub = lax.broadcasted_iota(jnp.int32, qt.shape, 0)
        zero = jnp.zeros_like(qt)
        qz_ref[:, r * tk:(r + 1) * tk] = jnp.where(sub < A_HEAD_DIM, qt, zero)
        qz_ref[:, tq + r * tk:tq + (r + 1) * tk] = jnp.where(sub >= A_HEAD_DIM, qt, zero)
    m_ref[...] = jnp.full_like(m_ref, NEG)
    acc_ref[...] = jnp.zeros_like(acc_ref)

    n_blocks = 2 * tq // ATTN_LANE_BLOCK
    lane_blocks = [slice(b * ATTN_LANE_BLOCK, (b + 1) * ATTN_LANE_BLOCK) for b in range(n_blocks)]

    def scores_block(j, buf, masked, b):
        lanes = lane_blocks[b]
        start = pl.multiple_of(j * tk, tk)
        kt = k_ref[pl.ds(start, tk), :]
        s = jnp.dot(kt, qz_ref[:, lanes], preferred_element_type=F32)
        if masked:
            kpos = j * tk + lax.broadcasted_iota(jnp.int32, s.shape, 0)
            col = b * ATTN_LANE_BLOCK + lax.broadcasted_iota(jnp.int32, s.shape, 1)
            qpos = i * tq + jnp.where(col >= tq, col - tq, col)
            s = jnp.where(kpos <= qpos, jnp.where(kpos >= N_PAD, s, NEG), NEG)
        s_refs[buf][:, lanes] = s
        mt_refs[buf][:, lanes] = jnp.max(s, axis=0, keepdims=True)

    ones_rows = jnp.ones((BF16_SUBLANES, tk), BF16)

    def accumulate_block(j, buf, b):
        lanes = lane_blocks[b]
        m_prev = m_ref[:, lanes]
        m_new = jnp.maximum(m_prev, mt_refs[buf][:, lanes])
        alpha = jnp.exp2(m_prev - m_new)
        p = jnp.exp2((s_refs[buf][:, lanes] - m_new).astype(BF16))
        m_ref[:, lanes] = m_new
        vt = jnp.concatenate([vt_ref[j], ones_rows], axis=0)
        pv = jnp.dot(vt, p, preferred_element_type=F32)
        acc_ref[:, lanes] = alpha * acc_ref[:, lanes] + pv

    def scores(j, buf, masked):
        for b in range(n_blocks):
            scores_block(j, buf, masked, b)

    def accumulate(j, buf):
        for b in range(n_blocks):
            accumulate_block(j, buf, b)

    def stage(j, masked_next):
        for buf in (0, 1):
            @pl.when(j % 2 == buf)
            def _():
                for b in range(n_blocks):
                    accumulate_block(j, buf, b)
                    scores_block(j + 1, 1 - buf, masked_next, b)

    n_diag = tq // tk
    n_k = (i + 1) * n_diag
    first_diag = n_k - n_diag
    scores(0, 0, True)

    def body(j, carry):
        stage(j, False)
        return carry

    def body_masked(j, carry):
        stage(j, True)
        return carry

    lax.fori_loop(0, first_diag - 1, body, 0)
    lax.fori_loop(jnp.maximum(first_diag - 1, 0), n_k - 1, body_masked, 0)

    for buf in (0, 1):
        @pl.when((n_k - 1) % 2 == buf)
        def _():
            accumulate(n_k - 1, buf)

    lam = (jnp.exp(jnp.sum(lq1_ref[...] * lk1_ref[...], axis=-1, keepdims=True))
           - jnp.exp(jnp.sum(lq2_ref[...] * lk2_ref[...], axis=-1, keepdims=True))
           + lam_init)
    out = acc_ref[0:A_VDIM, :] / acc_ref[A_VDIM:A_VDIM + 1, :]
    o = out[:, 0:tq] - lam * out[:, tq:2 * tq]
    o = o * lax.rsqrt(jnp.mean(o * o, axis=0, keepdims=True) + LN_EPS)
    o = o * g_ref[...] * (1.0 - lam_init)
    o_ref[...] = o.T.astype(BF16)


def _attn_call(p, q_t, v_t, lq1, lk1, lq2, lk2, norm_g, lam_init):
    n = p.shape[0]
    tq, tk = ATTN_TQ, ATTN_TK
    hpb = D_MODEL // A_VDIM
    vec = lambda d: pl.BlockSpec((1, d), lambda h, i: (0, 0))
    return pl.pallas_call(
        functools.partial(_attn_kernel, lam_init=lam_init),
        out_shape=jax.ShapeDtypeStruct((n, D_MODEL), BF16),
        grid=(A_HEADS, n // tq),
        in_specs=[
            pl.BlockSpec((tq // tk, A_VDIM, tk), lambda h, i: (i, h, 0)),
            pl.BlockSpec((n, A_VDIM), lambda h, i: (0, COL_K * hpb + h)),
            pl.BlockSpec((n // tk, A_VDIM, tk), lambda h, i: (0, h, 0)),
            vec(A_HEAD_DIM), vec(A_HEAD_DIM), vec(A_HEAD_DIM), vec(A_HEAD_DIM),
            pl.BlockSpec((A_VDIM, 1), lambda h, i: (0, 0)),
        ],
        out_specs=pl.BlockSpec((tq, A_VDIM), lambda h, i: (i, h)),
        scratch_shapes=[
            pltpu.VMEM((A_VDIM, 2 * tq), BF16),
            pltpu.VMEM((tk, 2 * tq), F32),
            pltpu.VMEM((tk, 2 * tq), F32),
            pltpu.VMEM((1, 2 * tq), F32),
            pltpu.VMEM((1, 2 * tq), F32),
            pltpu.VMEM((1, 2 * tq), F32),
            pltpu.VMEM((A_VDIM + BF16_SUBLANES, 2 * tq), F32),
        ],
        compiler_params=_params(("parallel", "arbitrary")),
        name="diff_attn",
    )(q_t, p, v_t, lq1.reshape(1, -1), lk1.reshape(1, -1), lq2.reshape(1, -1),
      lk2.reshape(1, -1), norm_g.reshape(-1, 1))


def _merge_kernel(ym_ref, ya_ref, ga_ref, gb_ref, h_ref, wm_ref, wa_ref, wo_ref,
                  g_ref, b_ref, wr_ref, rb_ref, h1_ref, comb_ref, *, alpha):
    bm = jnp.dot(ym_ref[...], wm_ref[...], preferred_element_type=F32)
    ba = jnp.dot(ya_ref[...], wa_ref[...], preferred_element_type=F32)
    merged = (_sigmoid(ga_ref[...].astype(F32)) * bm
              + _sigmoid(gb_ref[...].astype(F32)) * ba)
    proj = jnp.dot(merged.astype(BF16), wo_ref[...], preferred_element_type=F32)
    h1 = _layer_norm(alpha * h_ref[...] + proj, g_ref[...], b_ref[...])
    h1_ref[...] = h1

    h_hi = h1.astype(BF16)
    h_lo = (h1 - h_hi.astype(F32)).astype(BF16)
    r_hi = jnp.dot(h_hi, wr_ref[...], preferred_element_type=F32)
    r_lo = jnp.dot(h_lo, wr_ref[:, 0:LANES], preferred_element_type=F32)
    logits = (r_hi[:, 0:LANES] + r_lo) + r_hi[:, LANES:2 * LANES]
    lane = lax.broadcasted_iota(jnp.int32, logits.shape, 1)
    is_expert = lane < N_EXPERTS
    logits = jnp.where(is_expert, logits, NEG)
    ex = jnp.exp(logits - jnp.max(logits, axis=-1, keepdims=True))
    probs = ex / jnp.sum(ex, axis=-1, keepdims=True)
    sel = jnp.where(is_expert, probs + rb_ref[...], NEG)
    group = lane // EXPERTS_PER_GROUP
    best = jnp.max(jnp.where(group == 0, sel, NEG), axis=-1, keepdims=True)
    top_group = jnp.zeros_like(best, dtype=jnp.int32)
    for gidx in range(1, N_GROUPS):
        score = jnp.max(jnp.where(group == gidx, sel, NEG), axis=-1, keepdims=True)
        top_group = jnp.where(score > best, gidx, top_group)
        best = jnp.maximum(best, score)
    cand = jnp.where(group == top_group, sel, NEG)
    v1 = jnp.max(cand, axis=-1, keepdims=True)
    i1 = jnp.min(jnp.where(cand == v1, lane, LANES), axis=-1, keepdims=True)
    cand2 = jnp.where(lane == i1, NEG, cand)
    v2 = jnp.max(cand2, axis=-1, keepdims=True)
    i2 = jnp.min(jnp.where(cand2 == v2, lane, LANES), axis=-1, keepdims=True)
    p1 = jnp.sum(jnp.where(lane == i1, probs, 0.0), axis=-1, keepdims=True)
    p2 = jnp.sum(jnp.where(lane == i2, probs, 0.0), axis=-1, keepdims=True)
    tot = p1 + p2
    comb_ref[...] = (jnp.where(lane == i1, p1 / tot, 0.0)
                     + jnp.where(lane == i2, p2 / tot, 0.0))


def _merge_call(y_m, y_a, p, h, w_m, w_a, w_o, ln_g, ln_b, w_router, router_bias, alpha):
    n = h.shape[0]
    tm = MERGE_TILE
    full = pl.BlockSpec((D_MODEL, D_MODEL), lambda i: (0, 0))
    row = pl.BlockSpec((1, D_MODEL), lambda i: (0, 0))
    tile = lambda col: pl.BlockSpec((tm, D_MODEL), lambda i: (i, col))
    wr32 = jnp.zeros((D_MODEL, LANES), F32).at[:, :N_EXPERTS].set(w_router)
    wr_hi = wr32.astype(BF16)
    wr = jnp.concatenate([wr_hi, (wr32 - wr_hi.astype(F32)).astype(BF16)], axis=1)
    rb = jnp.zeros((1, LANES), F32).at[0, :N_EXPERTS].set(router_bias)
    return pl.pallas_call(
        functools.partial(_merge_kernel, alpha=alpha),
        out_shape=(jax.ShapeDtypeStruct((n, D_MODEL), F32),
                   jax.ShapeDtypeStruct((n, LANES), F32)),
        grid=(n // tm,),
        in_specs=[tile(0), tile(0), tile(COL_GA), tile(COL_GB), tile(0),
                  full, full, full, row, row,
                  pl.BlockSpec((D_MODEL, 2 * LANES), lambda i: (0, 0)),
                  pl.BlockSpec((1, LANES), lambda i: (0, 0))],
        out_specs=(tile(0), pl.BlockSpec((tm, LANES), lambda i: (i, 0))),
        compiler_params=_params(("parallel",)),
        name="merge_router",
    )(y_m, y_a, p, p, h, w_m, w_a, w_o, ln_g.reshape(1, -1), ln_b.reshape(1, -1), wr, rb)


def _moe_kernel(h_ref, combt_ref, wg_ref, wu_ref, wd_ref, g_ref, b_ref, o_ref,
                xb_ref, acc_ref, rankr_ref, *, alpha):
    e = pl.program_id(1)
    tm = MOE_TILE
    ns = MOE_SLOTS

    @pl.when(e == 0)
    def _():
        xb_ref[...] = h_ref[...].astype(BF16)
        acc_ref[...] = jnp.zeros_like(acc_ref)
        r_idx = lax.broadcasted_iota(jnp.int32, (tm, tm), 0)
        c_idx = lax.broadcasted_iota(jnp.int32, (tm, tm), 1)
        after = jnp.where(r_idx < c_idx, 1.0, 0.0).astype(BF16)
        rankr_ref[...] = jnp.dot(jnp.where(combt_ref[...] > 0.0, 1.0, 0.0).astype(BF16), after,
                                 preferred_element_type=F32)

    w_row = combt_ref[pl.ds(e, 1), :]
    routed_r = w_row > 0.0
    rank_r = rankr_ref[pl.ds(e, 1), :]
    count = jnp.sum(jnp.where(routed_r, 1.0, 0.0)).astype(jnp.int32)
    n_blocks = (count + ns - 1) // ns
    tn_dims = (((0,), (0,)), ((), ()))

    def block(b, carry):
        base = (b * ns).astype(F32)
        slot_r = lax.broadcasted_iota(jnp.int32, (ns, tm), 0).astype(F32) + base
        hit = routed_r & (rank_r == slot_r)
        sel = jnp.where(hit, 1.0, 0.0).astype(BF16)
        w_slot = jnp.sum(jnp.where(hit, w_row, 0.0), axis=-1, keepdims=True)
        xs = jnp.dot(sel, xb_ref[...], preferred_element_type=F32).astype(BF16)
        gate = jnp.dot(xs, wg_ref[0], preferred_element_type=F32)
        up = jnp.dot(xs, wu_ref[0], preferred_element_type=F32)
        mid = (gate * _sigmoid(gate) * up).astype(BF16)
        out = jnp.dot(mid, wd_ref[0], preferred_element_type=F32) * w_slot
        acc_ref[...] += lax.dot_general(sel, out.astype(BF16), tn_dims,
                                        preferred_element_type=F32)
        return carry

    lax.fori_loop(0, n_blocks, block, 0)

    @pl.when(e == N_EXPERTS - 1)
    def _():
        o_ref[...] = _layer_norm(alpha * h_ref[...] + acc_ref[...], g_ref[...], b_ref[...])


def _moe_call(h1, comb, w_gate, w_up, w_down, ln_g, ln_b, alpha):
    n = h1.shape[0]
    tm = MOE_TILE
    comb_t = comb[:, :N_EXPERTS].T
    wspec = pl.BlockSpec((1, D_MODEL, D_MODEL), lambda i, e: (e, 0, 0))
    row = pl.BlockSpec((1, D_MODEL), lambda i, e: (0, 0))
    return pl.pallas_call(
        functools.partial(_moe_kernel, alpha=alpha),
        out_shape=jax.ShapeDtypeStruct((n, D_MODEL), F32),
        grid=(n // tm, N_EXPERTS),
        in_specs=[pl.BlockSpec((tm, D_MODEL), lambda i, e: (i, 0),
                               pipeline_mode=pl.Buffered(1)),
                  pl.BlockSpec((N_EXPERTS, tm), lambda i, e: (0, i)),
                  wspec, wspec, wspec, row, row],
        out_specs=pl.BlockSpec((tm, D_MODEL), lambda i, e: (i, 0)),
        scratch_shapes=[pltpu.VMEM((tm, D_MODEL), BF16),
                        pltpu.VMEM((tm, D_MODEL), F32),
                        pltpu.VMEM((N_EXPERTS, tm), F32)],
        compiler_params=_params(("parallel", "arbitrary")),
        name="moe",
    )(h1, comb_t, w_gate, w_up, w_down, ln_g.reshape(1, -1), ln_b.reshape(1, -1))


def kernel(x, meta_tokens, ln_in_g, ln_in_b, w_in, conv_w, conv_b, wq_m, wk_m, wv_m, b_i, b_f, m_norm_g, lam_q1, lam_k1, lam_q2, lam_k2, a_norm_g, w_branch_m, w_branch_a, w_out, ln1_g, ln1_b, ln2_g, ln2_b, w_router, router_bias, w_gate_e, w_up_e, w_down_e):
    bsz, seq, dm = x.shape
    assert bsz == 1 and dm == D_MODEL
    depth = w_in.shape[0]
    alpha = float((2 * depth) ** 0.25)
    length = LEAD + seq
    n = -(-length // ROW_ALIGN) * ROW_ALIGN

    rows = jnp.concatenate([
        jnp.zeros((N_PAD, dm), F32), meta_tokens.astype(F32), x[0].astype(F32),
        jnp.zeros((n - length, dm), F32)], axis=0)
    h = _ln_call(rows, ln_in_g, ln_in_b)

    o_if = 2 * D_MODEL
    o_rest = o_if + 2 * M_HEADS
    for l in range(depth):
        w = w_in[l]
        q_scale = A_HEAD_DIM ** -0.5 * math.log2(math.e)
        blk = lambda b: w[:, o_rest + b * D_MODEL:o_rest + (b + 1) * D_MODEL]
        w_q, w_k, w_v, w_ga, w_gb = (blk(b) for b in range(5))
        w_main = jnp.concatenate([w[:, :o_if], w_k, w_ga, w_gb, w_q * q_scale, w_v],
                                 axis=1).astype(BF16)
        w_if = jnp.zeros((dm, LANES), F32).at[:, :2 * M_HEADS].set(w[:, o_if:o_rest]).astype(BF16)
        p, gif, q_t, v_t = _inproj_call(h, w_main, w_if)
        gif_t = gif[:, :2 * M_HEADS].T
        y_m = _mlstm_call(p, gif, gif_t, conv_w[l], conv_b[l], wq_m[l], wk_m[l], wv_m[l],
                          b_i[l], b_f[l], m_norm_g[l])
        lam_init = 0.8 - 0.6 * math.exp(-0.3 * l)
        y_a = _attn_call(p, q_t, v_t, lam_q1[l], lam_k1[l], lam_q2[l], lam_k2[l], a_norm_g[l],
                         lam_init)
        h1, comb = _merge_call(y_m, y_a, p, h, w_branch_m[l].astype(BF16),
                               w_branch_a[l].astype(BF16), w_out[l].astype(BF16),
                               ln1_g[l], ln1_b[l], w_router, router_bias, alpha)
        h = _moe_call(h1, comb, w_gate_e[l].astype(BF16), w_up_e[l].astype(BF16),
                      w_down_e[l].astype(BF16), ln2_g[l], ln2_b[l], alpha)
    return h[LEAD:length][None].astype(x.dtype)
```

```python
import functools
import math

import jax
import jax.numpy as jnp
from jax import lax
from jax.experimental import pallas as pl
from jax.experimental.pallas import tpu as pltpu

F32 = jnp.float32
BF16 = jnp.bfloat16

D_MODEL = 1024
N_META = 16
LEAD = 128
N_PAD = LEAD - N_META
M_HEADS = 4
M_HEAD_DIM = D_MODEL // M_HEADS
CONV_W = 4
A_HEADS = 8
A_HEAD_DIM = D_MODEL // (2 * A_HEADS)
A_VDIM = 2 * A_HEAD_DIM
N_EXPERTS = 16
N_GROUPS = 4
EXPERTS_PER_GROUP = N_EXPERTS // N_GROUPS
LN_EPS = 1e-5
NEG = -1e30

LANES = 128
BF16_SUBLANES = 16
VMEM_LIMIT_BYTES = 56 * 1024 * 1024

ROW_ALIGN = 1536
LN_TILE = 512
LN_PIECES = LN_TILE // LEAD
PROJ_TILE_M = 1536
PROJ_TILE_N = 1024
M_CHUNK = 256
ATTN_TQ = 1536
ATTN_TK = 768
ATTN_LANE_BLOCK = 256
MERGE_TILE = 512
MOE_TILE = 1536
MOE_SLOTS = 256

COL_XM, COL_OPRE, COL_K, COL_GA, COL_GB = range(5)
N_ROW_BLOCKS = 5


def _params(semantics):
    return pltpu.CompilerParams(dimension_semantics=semantics,
                                vmem_limit_bytes=VMEM_LIMIT_BYTES)


def _layer_norm(x, g, b):
    mu = jnp.mean(x, axis=-1, keepdims=True)
    xc = x - mu
    var = jnp.mean(xc * xc, axis=-1, keepdims=True)
    return xc * lax.rsqrt(var + LN_EPS) * g + b


def _sigmoid(x):
    return 1.0 / (1.0 + jnp.exp(-x))


def _ln_kernel(*refs, n_x_blocks):
    x_refs, (lead_ref, g_ref, b_ref, o_ref) = refs[:LN_PIECES], refs[LN_PIECES:]
    i = pl.program_id(0)
    for k, x_ref in enumerate(x_refs):
        blk = LN_PIECES * i - 1 + k
        rows = jnp.where(blk < 0, lead_ref[...], x_ref[...])
        rows = jnp.where(blk >= n_x_blocks, 0.0, rows)
        o_ref[k * LEAD:(k + 1) * LEAD, :] = _layer_norm(rows, g_ref[...], b_ref[...])


def _ln_call(x, lead, g, b, n):
    n_x_blocks = x.shape[0] // LEAD
    row = pl.BlockSpec((1, D_MODEL), lambda i: (0, 0))
    piece = lambda k: pl.BlockSpec(
        (LEAD, D_MODEL), lambda i: (jnp.clip(LN_PIECES * i - 1 + k, 0, n_x_blocks - 1), 0))
    return pl.pallas_call(
        functools.partial(_ln_kernel, n_x_blocks=n_x_blocks),
        out_shape=jax.ShapeDtypeStruct((n, D_MODEL), F32),
        grid=(n // LN_TILE,),
        in_specs=[piece(k) for k in range(LN_PIECES)]
        + [pl.BlockSpec((LEAD, D_MODEL), lambda i: (0, 0)), row, row],
        out_specs=pl.BlockSpec((LN_TILE, D_MODEL), lambda i: (i, 0)),
        compiler_params=_params(("parallel",)),
        name="ln_in",
    )(*([x] * LN_PIECES), lead, g.reshape(1, -1), b.reshape(1, -1))


def _inproj_kernel(h_ref, w_ref, wif_ref, p_ref, gif_ref, qt_ref, vt_ref, hb_ref):
    j = pl.program_id(1)
    tk = ATTN_TK

    @pl.when(j == 0)
    def _():
        hb_ref[...] = h_ref[...].astype(BF16)
        gif_ref[...] = jnp.dot(hb_ref[...], wif_ref[...], preferred_element_type=F32)

    def project():
        return jnp.dot(hb_ref[...], w_ref[...], preferred_element_type=F32)

    @pl.when(j < N_ROW_BLOCKS)
    def _():
        p_ref[...] = project().astype(BF16)

    def put_transposed(o_ref):
        res = project()
        for r in range(PROJ_TILE_M // tk):
            o_ref[r] = res[r * tk:(r + 1) * tk, :].T.astype(BF16)

    @pl.when(j == N_ROW_BLOCKS)
    def _():
        put_transposed(qt_ref)

    @pl.when(j == N_ROW_BLOCKS + 1)
    def _():
        put_transposed(vt_ref)


def _inproj_call(h, w_main, w_if):
    n = h.shape[0]
    tm, tk = PROJ_TILE_M, ATTN_TK
    t_shape = jax.ShapeDtypeStruct((n // tk, D_MODEL, tk), BF16)
    t_spec = pl.BlockSpec((tm // tk, D_MODEL, tk), lambda i, j: (i, 0, 0))
    return pl.pallas_call(
        _inproj_kernel,
        out_shape=(jax.ShapeDtypeStruct((n, N_ROW_BLOCKS * D_MODEL), BF16),
                   jax.ShapeDtypeStruct((n, LANES), F32), t_shape, t_shape),
        grid=(n // tm, w_main.shape[1] // PROJ_TILE_N),
        in_specs=[pl.BlockSpec((tm, D_MODEL), lambda i, j: (i, 0)),
                  pl.BlockSpec((D_MODEL, PROJ_TILE_N), lambda i, j: (0, j)),
                  pl.BlockSpec((D_MODEL, LANES), lambda i, j: (0, 0))],
        out_specs=(pl.BlockSpec((tm, PROJ_TILE_N),
                                lambda i, j: (i, jnp.minimum(j, N_ROW_BLOCKS - 1))),
                   pl.BlockSpec((tm, LANES), lambda i, j: (i, 0)), t_spec, t_spec),
        scratch_shapes=[pltpu.VMEM((tm, D_MODEL), BF16)],
        compiler_params=_params(("parallel", "arbitrary")),
        name="in_proj",
    )(h, w_main, w_if)


def _log_sigmoid(x):
    return jnp.minimum(x, 0.0) - jnp.log1p(jnp.exp(-jnp.abs(x)))


def _mlstm_kernel(xm_ref, halo_ref, opre_ref, gcol_ref, grow_ref, convw_ref, convb_ref,
                  wq_ref, wk_ref, wv_ref, bcol_ref, brow_ref, ng_ref,
                  y_ref, xs_ref, c_ref, n_ref, m_ref):
    c = pl.program_id(0)
    t = M_CHUNK
    halo = BF16_SUBLANES

    @pl.when(c == 0)
    def _():
        c_ref[...] = jnp.zeros_like(c_ref)
        n_ref[...] = jnp.zeros_like(n_ref)
        m_ref[...] = jnp.zeros_like(m_ref)

    pos_col = c * t + lax.broadcasted_iota(jnp.int32, (t, 1), 0)
    valid_col = pos_col >= N_PAD
    xm = jnp.where(valid_col, xm_ref[...].astype(F32), 0.0)
    halo_pos = c * t - halo + lax.broadcasted_iota(jnp.int32, (halo, 1), 0)
    xs_ref[0:halo, :] = jnp.where(halo_pos >= N_PAD, halo_ref[...].astype(F32), 0.0)
    xs_ref[halo:halo + t, :] = xm
    conv = xm * convw_ref[CONV_W - 1:CONV_W, :] + convb_ref[...]
    for k in range(CONV_W - 1):
        off = halo - (CONV_W - 1) + k
        conv = conv + xs_ref[off:off + t, :] * convw_ref[k:k + 1, :]
    xc = conv * _sigmoid(conv)

    gc = gcol_ref[...] + bcol_ref[...]
    lane = lax.broadcasted_iota(jnp.int32, gc.shape, 1)
    lg_col = jnp.where(lane < M_HEADS,
                       jnp.where(valid_col, gc, NEG),
                       jnp.where(valid_col, _log_sigmoid(gc), 0.0))
    gr = grow_ref[...] + brow_ref[:, 0:1]
    valid_row = (c * t + lax.broadcasted_iota(jnp.int32, (1, t), 1)) >= N_PAD
    sub = lax.broadcasted_iota(jnp.int32, gr.shape, 0)
    lg_row = jnp.where(sub < M_HEADS,
                       jnp.where(valid_row, gr, NEG),
                       jnp.where(valid_row, _log_sigmoid(gr), 0.0))
    r_idx = lax.broadcasted_iota(jnp.int32, (t, t), 0)
    c_idx = lax.broadcasted_iota(jnp.int32, (t, t), 1)
    causal = c_idx <= r_idx
    tril = jnp.where(causal, 1.0, 0.0).astype(F32)
    triu = jnp.where(r_idx <= c_idx, 1.0, 0.0).astype(F32)
    cum_col = jnp.dot(tril, lg_col, preferred_element_type=F32, precision=lax.Precision.HIGHEST)
    cum_row = jnp.dot(lg_row, triu, preferred_element_type=F32, precision=lax.Precision.HIGHEST)

    nt_dims = (((1,), (1,)), ((), ()))
    tn_dims = (((0,), (0,)), ((), ()))
    heads = range(M_HEADS)
    sls = [slice(h * M_HEAD_DIM, (h + 1) * M_HEAD_DIM) for h in heads]

    q, k, qb, kb, vb = [], [], [], [], []
    for h in heads:
        xch = xc[:, sls[h]].astype(BF16)
        q.append(jnp.dot(xch, wq_ref[h], preferred_element_type=F32))
        k.append(jnp.dot(xch, wk_ref[h], preferred_element_type=F32))
        v = jnp.dot(xm[:, sls[h]].astype(BF16), wv_ref[h], preferred_element_type=F32)
        qb.append(q[h].astype(BF16))
        kb.append(k[h].astype(BF16))
        vb.append(v.astype(BF16))

    b_c = [cum_col[:, M_HEADS + h:M_HEADS + h + 1] for h in heads]
    li_c = [lg_col[:, h:h + 1] for h in heads]
    m_prev = [m_ref[h][:, 0:1] for h in heads]
    g, m_t, w_intra, scale_inter = [], [], [], []
    for h in heads:
        li_r = lg_row[h:h + 1, :]
        b_r = cum_row[M_HEADS + h:M_HEADS + h + 1, :]
        g.append(b_r[:, t - 1:t])
        d_mat = jnp.where(causal, b_c[h] - b_r + li_r, NEG)
        inter = b_c[h] + m_prev[h]
        m_t.append(jnp.maximum(inter, jnp.max(d_mat, axis=-1, keepdims=True)))
        s = lax.dot_general(qb[h], kb[h], nt_dims, preferred_element_type=F32)
        w_intra.append(s * jnp.exp(d_mat - m_t[h]))
        scale_inter.append(jnp.exp(inter - m_t[h]))

    hh = []
    for h in heads:
        num = (jnp.dot(w_intra[h].astype(BF16), vb[h], preferred_element_type=F32)
               + scale_inter[h] * jnp.dot(qb[h], c_ref[h].astype(BF16),
                                          preferred_element_type=F32))
        den = (jnp.sum(w_intra[h], axis=-1, keepdims=True)
               + scale_inter[h] * jnp.sum(q[h] * n_ref[h], axis=-1, keepdims=True))
        hh.append(num / jnp.maximum(jnp.abs(den), jnp.exp(-m_t[h])))

    for h in heads:
        a = g[h] - b_c[h] + li_c[h]
        m_new = jnp.maximum(g[h] + m_prev[h], jnp.max(a, axis=0, keepdims=True))
        w_state = jnp.exp(a - m_new)
        decay = jnp.exp(g[h] + m_prev[h] - m_new)
        kw = k[h] * w_state
        c_ref[h] = decay * c_ref[h] + lax.dot_general(kw.astype(BF16), vb[h], tn_dims,
                                                      preferred_element_type=F32)
        n_ref[h] = decay * n_ref[h] + jnp.sum(kw, axis=0, keepdims=True)
        m_ref[h] = jnp.broadcast_to(m_new, (1, LANES))

    for h in heads:
        gated = _sigmoid(opre_ref[:, sls[h]].astype(F32)) * hh[h]
        mu = jnp.mean(gated, axis=-1, keepdims=True)
        hc = gated - mu
        var = jnp.mean(hc * hc, axis=-1, keepdims=True)
        y_ref[:, sls[h]] = (hc * lax.rsqrt(var + LN_EPS) * ng_ref[:, sls[h]]).astype(BF16)


def _mlstm_call(p, gif, gif_t, conv_w, conv_b, wq, wk, wv, b_i, b_f, norm_g):
    n = p.shape[0]
    t = M_CHUNK
    halo = BF16_SUBLANES
    bias = jnp.concatenate([b_i, b_f]).astype(F32)
    bias_col = jnp.zeros((1, LANES), F32).at[0, :2 * M_HEADS].set(bias)
    bias_row = jnp.broadcast_to(bias[:, None], (2 * M_HEADS, LANES))
    const = lambda *shape: pl.BlockSpec(shape, lambda c: (0,) * len(shape))
    return pl.pallas_call(
        _mlstm_kernel,
        out_shape=jax.ShapeDtypeStruct((n, D_MODEL), BF16),
        grid=(n // t,),
        in_specs=[
            pl.BlockSpec((t, D_MODEL), lambda c: (c, COL_XM)),
            pl.BlockSpec((halo, D_MODEL),
                         lambda c: (jnp.maximum(c * (t // halo) - 1, 0), COL_XM)),
            pl.BlockSpec((t, D_MODEL), lambda c: (c, COL_OPRE)),
            pl.BlockSpec((t, LANES), lambda c: (c, 0)),
            pl.BlockSpec((2 * M_HEADS, t), lambda c: (0, c)),
            const(CONV_W, D_MODEL),
            const(1, D_MODEL),
            const(M_HEADS, M_HEAD_DIM, M_HEAD_DIM),
            const(M_HEADS, M_HEAD_DIM, M_HEAD_DIM),
            const(M_HEADS, M_HEAD_DIM, M_HEAD_DIM),
            const(1, LANES),
            const(2 * M_HEADS, LANES),
            const(1, D_MODEL),
        ],
        out_specs=pl.BlockSpec((t, D_MODEL), lambda c: (c, 0)),
        scratch_shapes=[
            pltpu.VMEM((t + halo, D_MODEL), F32),
            pltpu.VMEM((M_HEADS, M_HEAD_DIM, M_HEAD_DIM), F32),
            pltpu.VMEM((M_HEADS, 1, M_HEAD_DIM), F32),
            pltpu.VMEM((M_HEADS, 1, LANES), F32),
        ],
        compiler_params=_params(("arbitrary",)),
        name="mlstm",
    )(p, p, p, gif, gif_t, conv_w, conv_b.reshape(1, -1),
      wq.astype(BF16), (wk * M_HEAD_DIM ** -0.5).astype(BF16), wv.astype(BF16),
      bias_col, bias_row,
      norm_g.reshape(1, -1))


def _attn_kernel(qt_ref, k_ref, vt_ref, lq1_ref, lk1_ref, lq2_ref, lk2_ref, g_ref,
                 o_ref, qz_ref, s0_ref, s1_ref, mt0_ref, mt1_ref, m_ref, acc_ref,
                 *, lam_init):
    i = pl.program_id(1)
    tq, tk = ATTN_TQ, ATTN_TK
    s_refs = (s0_ref, s1_ref)
    mt_refs = (mt0_ref, mt1_ref)

    for r in range(tq // tk):
        qt = qt_ref[r]
        sub = lax.broadcasted_iota(jnp.int32, qt.shape, 0)
        zero = jnp.zeros_like(qt)
        qz_ref[:, r * tk:(r + 1) * tk] = jnp.where(sub < A_HEAD_DIM, qt, zero)
        qz_ref[:, tq + r * tk:tq + (r + 1) * tk] = jnp.where(sub >= A_HEAD_DIM, qt, zero)
    m_ref[...] = jnp.full_like(m_ref, NEG)
    acc_ref[...] = jnp.zeros_like(acc_ref)

    n_blocks = 2 * tq // ATTN_LANE_BLOCK
    lane_blocks = [slice(b * ATTN_LANE_BLOCK, (b + 1) * ATTN_LANE_BLOCK) for b in range(n_blocks)]

    def scores_block(j, buf, masked, b):
        lanes = lane_blocks[b]
        start = pl.multiple_of(j * tk, tk)
        kt = k_ref[pl.ds(start, tk), :]
        s = jnp.dot(kt, qz_ref[:, lanes], preferred_element_type=F32)
        if masked:
            kpos = j * tk + lax.broadcasted_iota(jnp.int32, s.shape, 0)
            col = b * ATTN_LANE_BLOCK + lax.broadcasted_iota(jnp.int32, s.shape, 1)
            qpos = i * tq + jnp.where(col >= tq, col - tq, col)
            s = jnp.where(kpos <= qpos, jnp.where(kpos >= N_PAD, s, NEG), NEG)
        s_refs[buf][:, lanes] = s
        mt_refs[buf][:, lanes] = jnp.max(s, axis=0, keepdims=True)

    ones_rows = jnp.ones((BF16_SUBLANES, tk), BF16)

    def accumulate_block(j, buf, b):
        lanes = lane_blocks[b]
        m_prev = m_ref[:, lanes]
        m_new = jnp.maximum(m_prev, mt_refs[buf][:, lanes])
        alpha = jnp.exp2(m_prev - m_new)
        p = jnp.exp2((s_refs[buf][:, lanes] - m_new).astype(BF16))
        m_ref[:, lanes] = m_new
        vt = jnp.concatenate([vt_ref[j], ones_rows], axis=0)
        pv = jnp.dot(vt, p, preferred_element_type=F32)
        acc_ref[:, lanes] = alpha * acc_ref[:, lanes] + pv

    def scores(j, buf, masked):
        for b in range(n_blocks):
            scores_block(j, buf, masked, b)

    def accumulate(j, buf):
        for b in range(n_blocks):
            accumulate_block(j, buf, b)

    def stage(j, masked_next):
        for buf in (0, 1):
            @pl.when(j % 2 == buf)
            def _():
                for b in range(n_blocks):
                    accumulate_block(j, buf, b)
                    scores_block(j + 1, 1 - buf, masked_next, b)

    n_diag = tq // tk
    n_k = (i + 1) * n_diag
    first_diag = n_k - n_diag
    scores(0, 0, True)

    def body(j, carry):
        stage(j, False)
        return carry

    def body_masked(j, carry):
        stage(j, True)
        return carry

    lax.fori_loop(0, first_diag - 1, body, 0)
    lax.fori_loop(jnp.maximum(first_diag - 1, 0), n_k - 1, body_masked, 0)

    for buf in (0, 1):
        @pl.when((n_k - 1) % 2 == buf)
        def _():
            accumulate(n_k - 1, buf)

    lam = (jnp.exp(jnp.sum(lq1_ref[...] * lk1_ref[...], axis=-1, keepdims=True))
           - jnp.exp(jnp.sum(lq2_ref[...] * lk2_ref[...], axis=-1, keepdims=True))
           + lam_init)
    out = acc_ref[0:A_VDIM, :] / acc_ref[A_VDIM:A_VDIM + 1, :]
    o = out[:, 0:tq] - lam * out[:, tq:2 * tq]
    o = o * lax.rsqrt(jnp.mean(o * o, axis=0, keepdims=True) + LN_EPS)
    o = o * g_ref[...] * (1.0 - lam_init)
    o_ref[...] = o.T.astype(BF16)


def _attn_call(p, q_t, v_t, lq1, lk1, lq2, lk2, norm_g, lam_init):
    n = p.shape[0]
    tq, tk = ATTN_TQ, ATTN_TK
    hpb = D_MODEL // A_VDIM
    vec = lambda d: pl.BlockSpec((1, d), lambda h, i: (0, 0))
    return pl.pallas_call(
        functools.partial(_attn_kernel, lam_init=lam_init),
        out_shape=jax.ShapeDtypeStruct((n, D_MODEL), BF16),
        grid=(A_HEADS, n // tq),
        in_specs=[
            pl.BlockSpec((tq // tk, A_VDIM, tk), lambda h, i: (i, h, 0)),
            pl.BlockSpec((n, A_VDIM), lambda h, i: (0, COL_K * hpb + h)),
            pl.BlockSpec((n // tk, A_VDIM, tk), lambda h, i: (0, h, 0)),
            vec(A_HEAD_DIM), vec(A_HEAD_DIM), vec(A_HEAD_DIM), vec(A_HEAD_DIM),
            pl.BlockSpec((A_VDIM, 1), lambda h, i: (0, 0)),
        ],
        out_specs=pl.BlockSpec((tq, A_VDIM), lambda h, i: (i, h)),
        scratch_shapes=[
            pltpu.VMEM((A_VDIM, 2 * tq), BF16),
            pltpu.VMEM((tk, 2 * tq), F32),
            pltpu.VMEM((tk, 2 * tq), F32),
            pltpu.VMEM((1, 2 * tq), F32),
            pltpu.VMEM((1, 2 * tq), F32),
            pltpu.VMEM((1, 2 * tq), F32),
            pltpu.VMEM((A_VDIM + BF16_SUBLANES, 2 * tq), F32),
        ],
        compiler_params=_params(("parallel", "arbitrary")),
        name="diff_attn",
    )(q_t, p, v_t, lq1.reshape(1, -1), lk1.reshape(1, -1), lq2.reshape(1, -1),
      lk2.reshape(1, -1), norm_g.reshape(-1, 1))


def _merge_kernel(ym_ref, ya_ref, ga_ref, gb_ref, h_ref, wm_ref, wa_ref, wo_ref,
                  g_ref, b_ref, wr_ref, rb_ref, h1_ref, comb_ref, *, alpha):
    bm = jnp.dot(ym_ref[...], wm_ref[...], preferred_element_type=F32)
    ba = jnp.dot(ya_ref[...], wa_ref[...], preferred_element_type=F32)
    merged = (_sigmoid(ga_ref[...].astype(F32)) * bm
              + _sigmoid(gb_ref[...].astype(F32)) * ba)
    proj = jnp.dot(merged.astype(BF16), wo_ref[...], preferred_element_type=F32)
    h1 = _layer_norm(alpha * h_ref[...] + proj, g_ref[...], b_ref[...])
    h1_ref[...] = h1

    h_hi = h1.astype(BF16)
    h_lo = (h1 - h_hi.astype(F32)).astype(BF16)
    r_hi = jnp.dot(h_hi, wr_ref[...], preferred_element_type=F32)
    r_lo = jnp.dot(h_lo, wr_ref[:, 0:LANES], preferred_element_type=F32)
    logits = (r_hi[:, 0:LANES] + r_lo) + r_hi[:, LANES:2 * LANES]
    lane = lax.broadcasted_iota(jnp.int32, logits.shape, 1)
    is_expert = lane < N_EXPERTS
    logits = jnp.where(is_expert, logits, NEG)
    ex = jnp.exp(logits - jnp.max(logits, axis=-1, keepdims=True))
    probs = ex / jnp.sum(ex, axis=-1, keepdims=True)
    sel = jnp.where(is_expert, probs + rb_ref[...], NEG)
    group = lane // EXPERTS_PER_GROUP
    best = jnp.max(jnp.where(group == 0, sel, NEG), axis=-1, keepdims=True)
    top_group = jnp.zeros_like(best, dtype=jnp.int32)
    for gidx in range(1, N_GROUPS):
        score = jnp.max(jnp.where(group == gidx, sel, NEG), axis=-1, keepdims=True)
        top_group = jnp.where(score > best, gidx, top_group)
        best = jnp.maximum(best, score)
    cand = jnp.where(group == top_group, sel, NEG)
    v1 = jnp.max(cand, axis=-1, keepdims=True)
    i1 = jnp.min(jnp.where(cand == v1, lane, LANES), axis=-1, keepdims=True)
    cand2 = jnp.where(lane == i1, NEG, cand)
    v2 = jnp.max(cand2, axis=-1, keepdims=True)
    i2 = jnp.min(jnp.where(cand2 == v2, lane, LANES), axis=-1, keepdims=True)
    p1 = jnp.sum(jnp.where(lane == i1, probs, 0.0), axis=-1, keepdims=True)
    p2 = jnp.sum(jnp.where(lane == i2, probs, 0.0), axis=-1, keepdims=True)
    tot = p1 + p2
    comb_ref[...] = (jnp.where(lane == i1, p1 / tot, 0.0)
                     + jnp.where(lane == i2, p2 / tot, 0.0))


def _merge_call(y_m, y_a, p, h, w_m, w_a, w_o, ln_g, ln_b, w_router, router_bias, alpha):
    n = h.shape[0]
    tm = MERGE_TILE
    full = pl.BlockSpec((D_MODEL, D_MODEL), lambda i: (0, 0))
    row = pl.BlockSpec((1, D_MODEL), lambda i: (0, 0))
    tile = lambda col: pl.BlockSpec((tm, D_MODEL), lambda i: (i, col))
    wr32 = jnp.zeros((D_MODEL, LANES), F32).at[:, :N_EXPERTS].set(w_router)
    wr_hi = wr32.astype(BF16)
    wr = jnp.concatenate([wr_hi, (wr32 - wr_hi.astype(F32)).astype(BF16)], axis=1)
    rb = jnp.zeros((1, LANES), F32).at[0, :N_EXPERTS].set(router_bias)
    return pl.pallas_call(
        functools.partial(_merge_kernel, alpha=alpha),
        out_shape=(jax.ShapeDtypeStruct((n, D_MODEL), F32),
                   jax.ShapeDtypeStruct((n, LANES), F32)),
        grid=(n // tm,),
        in_specs=[tile(0), tile(0), tile(COL_GA), tile(COL_GB), tile(0),
                  full, full, full, row, row,
                  pl.BlockSpec((D_MODEL, 2 * LANES), lambda i: (0, 0)),
                  pl.BlockSpec((1, LANES), lambda i: (0, 0))],
        out_specs=(tile(0), pl.BlockSpec((tm, LANES), lambda i: (i, 0))),
        compiler_params=_params(("parallel",)),
        name="merge_router",
    )(y_m, y_a, p, p, h, w_m, w_a, w_o, ln_g.reshape(1, -1), ln_b.reshape(1, -1), wr, rb)


def _moe_kernel(h_ref, combt_ref, wg_ref, wu_ref, wd_ref, g_ref, b_ref, o_ref,
                xb_ref, acc_ref, rankr_ref, *, alpha):
    e = pl.program_id(1)
    tm = MOE_TILE
    ns = MOE_SLOTS

    @pl.when(e == 0)
    def _():
        xb_ref[...] = h_ref[...].astype(BF16)
        acc_ref[...] = jnp.zeros_like(acc_ref)
        r_idx = lax.broadcasted_iota(jnp.int32, (tm, tm), 0)
        c_idx = lax.broadcasted_iota(jnp.int32, (tm, tm), 1)
        after = jnp.where(r_idx < c_idx, 1.0, 0.0).astype(BF16)
        rankr_ref[...] = jnp.dot(jnp.where(combt_ref[...] > 0.0, 1.0, 0.0).astype(BF16), after,
                                 preferred_element_type=F32)

    w_row = combt_ref[pl.ds(e, 1), :]
    routed_r = w_row > 0.0
    rank_r = rankr_ref[pl.ds(e, 1), :]
    count = jnp.sum(jnp.where(routed_r, 1.0, 0.0)).astype(jnp.int32)
    n_blocks = (count + ns - 1) // ns
    tn_dims = (((0,), (0,)), ((), ()))

    def block(b, carry):
        base = (b * ns).astype(F32)
        slot_r = lax.broadcasted_iota(jnp.int32, (ns, tm), 0).astype(F32) + base
        hit = routed_r & (rank_r == slot_r)
        sel = jnp.where(hit, 1.0, 0.0).astype(BF16)
        w_slot = jnp.sum(jnp.where(hit, w_row, 0.0), axis=-1, keepdims=True)
        xs = jnp.dot(sel, xb_ref[...], preferred_element_type=F32).astype(BF16)
        gate = jnp.dot(xs, wg_ref[0], preferred_element_type=F32)
        up = jnp.dot(xs, wu_ref[0], preferred_element_type=F32)
        mid = (gate * _sigmoid(gate) * up).astype(BF16)
        out = jnp.dot(mid, wd_ref[0], preferred_element_type=F32) * w_slot
        acc_ref[...] += lax.dot_general(sel, out.astype(BF16), tn_dims,
                                        preferred_element_type=F32)
        return carry

    lax.fori_loop(0, n_blocks, block, 0)

    @pl.when(e == N_EXPERTS - 1)
    def _():
        o_ref[...] = _layer_norm(alpha * h_ref[...] + acc_ref[...], g_ref[...], b_ref[...])


def _moe_call(h1, comb, w_gate, w_up, w_down, ln_g, ln_b, alpha):
    n = h1.shape[0]
    tm = MOE_TILE
    comb_t = comb[:, :N_EXPERTS].T
    wspec = pl.BlockSpec((1, D_MODEL, D_MODEL), lambda i, e: (e, 0, 0))
    row = pl.BlockSpec((1, D_MODEL), lambda i, e: (0, 0))
    return pl.pallas_call(
        functools.partial(_moe_kernel, alpha=alpha),
        out_shape=jax.ShapeDtypeStruct((n, D_MODEL), F32),
        grid=(n // tm, N_EXPERTS),
        in_specs=[pl.BlockSpec((tm, D_MODEL), lambda i, e: (i, 0),
                               pipeline_mode=pl.Buffered(1)),
                  pl.BlockSpec((N_EXPERTS, tm), lambda i, e: (0, i)),
                  wspec, wspec, wspec, row, row],
        out_specs=pl.BlockSpec((tm, D_MODEL), lambda i, e: (i, 0)),
        scratch_shapes=[pltpu.VMEM((tm, D_MODEL), BF16),
                        pltpu.VMEM((tm, D_MODEL), F32),
                        pltpu.VMEM((N_EXPERTS, tm), F32)],
        compiler_params=_params(("parallel", "arbitrary")),
        name="moe",
    )(h1, comb_t, w_gate, w_up, w_down, ln_g.reshape(1, -1), ln_b.reshape(1, -1))


def kernel(x, meta_tokens, ln_in_g, ln_in_b, w_in, conv_w, conv_b, wq_m, wk_m, wv_m, b_i, b_f, m_norm_g, lam_q1, lam_k1, lam_q2, lam_k2, a_norm_g, w_branch_m, w_branch_a, w_out, ln1_g, ln1_b, ln2_g, ln2_b, w_router, router_bias, w_gate_e, w_up_e, w_down_e):
    bsz, seq, dm = x.shape
    assert bsz == 1 and dm == D_MODEL
    depth = w_in.shape[0]
    alpha = float((2 * depth) ** 0.25)
    length = LEAD + seq
    n = -(-length // ROW_ALIGN) * ROW_ALIGN

    assert seq % LEAD == 0
    lead = jnp.concatenate([jnp.zeros((N_PAD, dm), F32), meta_tokens.astype(F32)], axis=0)
    h = _ln_call(x[0].astype(F32), lead, ln_in_g, ln_in_b, n)

    o_if = 2 * D_MODEL
    o_rest = o_if + 2 * M_HEADS
    for l in range(depth):
        w = w_in[l]
        q_scale = A_HEAD_DIM ** -0.5 * math.log2(math.e)
        blk = lambda b: w[:, o_rest + b * D_MODEL:o_rest + (b + 1) * D_MODEL]
        w_q, w_k, w_v, w_ga, w_gb = (blk(b) for b in range(5))
        w_main = jnp.concatenate([w[:, :o_if], w_k, w_ga, w_gb, w_q * q_scale, w_v],
                                 axis=1).astype(BF16)
        w_if = jnp.zeros((dm, LANES), F32).at[:, :2 * M_HEADS].set(w[:, o_if:o_rest]).astype(BF16)
        p, gif, q_t, v_t = _inproj_call(h, w_main, w_if)
        gif_t = gif[:, :2 * M_HEADS].T
        y_m = _mlstm_call(p, gif, gif_t, conv_w[l], conv_b[l], wq_m[l], wk_m[l], wv_m[l],
                          b_i[l], b_f[l], m_norm_g[l])
        lam_init = 0.8 - 0.6 * math.exp(-0.3 * l)
        y_a = _attn_call(p, q_t, v_t, lam_q1[l], lam_k1[l], lam_q2[l], lam_k2[l], a_norm_g[l],
                         lam_init)
        h1, comb = _merge_call(y_m, y_a, p, h, w_branch_m[l].astype(BF16),
                               w_branch_a[l].astype(BF16), w_out[l].astype(BF16),
                               ln1_g[l], ln1_b[l], w_router, router_bias, alpha)
        h = _moe_call(h1, comb, w_gate_e[l].astype(BF16), w_up_e[l].astype(BF16),
                      w_down_e[l].astype(BF16), ln2_g[l], ln2_b[l], alpha)
    return h[LEAD:length][None].astype(x.dtype)
```

```python
import functools
import math

import jax
import jax.numpy as jnp
from jax import lax
from jax.experimental import pallas as pl
from jax.experimental.pallas import tpu as pltpu

F32 = jnp.float32
BF16 = jnp.bfloat16

D_MODEL = 1024
N_META = 16
LEAD = 128
N_PAD = LEAD - N_META
M_HEADS = 4
M_HEAD_DIM = D_MODEL // M_HEADS
CONV_W = 4
A_HEADS = 8
A_HEAD_DIM = D_MODEL // (2 * A_HEADS)
A_VDIM = 2 * A_HEAD_DIM
N_EXPERTS = 16
N_GROUPS = 4
EXPERTS_PER_GROUP = N_EXPERTS // N_GROUPS
LN_EPS = 1e-5
NEG = -1e30

LANES = 128
BF16_SUBLANES = 16
VMEM_LIMIT_BYTES = 56 * 1024 * 1024

ROW_ALIGN = 1536
LN_TILE = 512
LN_PIECES = LN_TILE // LEAD
PROJ_TILE_M = 1536
PROJ_TILE_N = 1024
M_CHUNK = 256
ATTN_TQ = 1536
ATTN_TK = 768
ATTN_LANE_BLOCK = 256
MERGE_TILE = 512
MOE_TILE = 1536
MOE_SLOTS = 256

COL_XM, COL_OPRE, COL_K, COL_GA, COL_GB = range(5)
N_ROW_BLOCKS = 5


def _params(semantics):
    return pltpu.CompilerParams(dimension_semantics=semantics,
                                vmem_limit_bytes=VMEM_LIMIT_BYTES)


def _layer_norm(x, g, b):
    mu = jnp.mean(x, axis=-1, keepdims=True)
    xc = x - mu
    var = jnp.mean(xc * xc, axis=-1, keepdims=True)
    return xc * lax.rsqrt(var + LN_EPS) * g + b


def _sigmoid(x):
    return 1.0 / (1.0 + jnp.exp(-x))


def _ln_kernel(*refs, n_x_blocks):
    x_refs, (lead_ref, g_ref, b_ref, o_ref) = refs[:LN_PIECES], refs[LN_PIECES:]
    i = pl.program_id(0)
    for k, x_ref in enumerate(x_refs):
        blk = LN_PIECES * i - 1 + k
        rows = jnp.where(blk < 0, lead_ref[...], x_ref[...])
        rows = jnp.where(blk >= n_x_blocks, 0.0, rows)
        o_ref[k * LEAD:(k + 1) * LEAD, :] = _layer_norm(rows, g_ref[...], b_ref[...])


def _ln_call(x, lead, g, b, n):
    n_x_blocks = x.shape[0] // LEAD
    row = pl.BlockSpec((1, D_MODEL), lambda i: (0, 0))
    piece = lambda k: pl.BlockSpec(
        (LEAD, D_MODEL), lambda i: (jnp.clip(LN_PIECES * i - 1 + k, 0, n_x_blocks - 1), 0))
    return pl.pallas_call(
        functools.partial(_ln_kernel, n_x_blocks=n_x_blocks),
        out_shape=jax.ShapeDtypeStruct((n, D_MODEL), F32),
        grid=(n // LN_TILE,),
        in_specs=[piece(k) for k in range(LN_PIECES)]
        + [pl.BlockSpec((LEAD, D_MODEL), lambda i: (0, 0)), row, row],
        out_specs=pl.BlockSpec((LN_TILE, D_MODEL), lambda i: (i, 0)),
        compiler_params=_params(("parallel",)),
        name="ln_in",
    )(*([x] * LN_PIECES), lead, g.reshape(1, -1), b.reshape(1, -1))


def _inproj_kernel(h_ref, w_ref, wif_ref, p_ref, gif_ref, qt_ref, vt_ref, hb_ref):
    j = pl.program_id(1)
    tk = ATTN_TK

    @pl.when(j == 0)
    def _():
        hb_ref[...] = h_ref[...].astype(BF16)
        gif_ref[...] = jnp.dot(hb_ref[...], wif_ref[...], preferred_element_type=F32)

    def project():
        return jnp.dot(hb_ref[...], w_ref[...], preferred_element_type=F32)

    @pl.when(j < N_ROW_BLOCKS)
    def _():
        p_ref[...] = project().astype(BF16)

    def put_transposed(o_ref):
        res = project()
        for r in range(PROJ_TILE_M // tk):
            o_ref[r] = res[r * tk:(r + 1) * tk, :].T.astype(BF16)

    @pl.when(j == N_ROW_BLOCKS)
    def _():
        put_transposed(qt_ref)

    @pl.when(j == N_ROW_BLOCKS + 1)
    def _():
        put_transposed(vt_ref)


def _inproj_call(h, w_main, w_if):
    n = h.shape[0]
    tm, tk = PROJ_TILE_M, ATTN_TK
    t_shape = jax.ShapeDtypeStruct((n // tk, D_MODEL, tk), BF16)
    t_spec = pl.BlockSpec((tm // tk, D_MODEL, tk), lambda i, j: (i, 0, 0))
    return pl.pallas_call(
        _inproj_kernel,
        out_shape=(jax.ShapeDtypeStruct((n, N_ROW_BLOCKS * D_MODEL), BF16),
                   jax.ShapeDtypeStruct((n, LANES), F32), t_shape, t_shape),
        grid=(n // tm, w_main.shape[1] // PROJ_TILE_N),
        in_specs=[pl.BlockSpec((tm, D_MODEL), lambda i, j: (i, 0)),
                  pl.BlockSpec((D_MODEL, PROJ_TILE_N), lambda i, j: (0, j)),
                  pl.BlockSpec((D_MODEL, LANES), lambda i, j: (0, 0))],
        out_specs=(pl.BlockSpec((tm, PROJ_TILE_N),
                                lambda i, j: (i, jnp.minimum(j, N_ROW_BLOCKS - 1))),
                   pl.BlockSpec((tm, LANES), lambda i, j: (i, 0)), t_spec, t_spec),
        scratch_shapes=[pltpu.VMEM((tm, D_MODEL), BF16)],
        compiler_params=_params(("parallel", "arbitrary")),
        name="in_proj",
    )(h, w_main, w_if)


def _log_sigmoid(x):
    return jnp.minimum(x, 0.0) - jnp.log1p(jnp.exp(-jnp.abs(x)))


def _mlstm_kernel(xm_ref, halo_ref, opre_ref, gcol_ref, grow_ref, convw_ref, convb_ref,
                  wq_ref, wk_ref, wv_ref, bcol_ref, brow_ref, ng_ref,
                  y_ref, xs_ref, c_ref, n_ref, m_ref):
    c = pl.program_id(0)
    t = M_CHUNK
    halo = BF16_SUBLANES

    @pl.when(c == 0)
    def _():
        c_ref[...] = jnp.zeros_like(c_ref)
        n_ref[...] = jnp.zeros_like(n_ref)
        m_ref[...] = jnp.zeros_like(m_ref)

    pos_col = c * t + lax.broadcasted_iota(jnp.int32, (t, 1), 0)
    valid_col = pos_col >= N_PAD
    xm = jnp.where(valid_col, xm_ref[...].astype(F32), 0.0)
    halo_pos = c * t - halo + lax.broadcasted_iota(jnp.int32, (halo, 1), 0)
    xs_ref[0:halo, :] = jnp.where(halo_pos >= N_PAD, halo_ref[...].astype(F32), 0.0)
    xs_ref[halo:halo + t, :] = xm
    conv = xm * convw_ref[CONV_W - 1:CONV_W, :] + convb_ref[...]
    for k in range(CONV_W - 1):
        off = halo - (CONV_W - 1) + k
        conv = conv + xs_ref[off:off + t, :] * convw_ref[k:k + 1, :]
    xc = conv * _sigmoid(conv)

    gc = gcol_ref[...] + bcol_ref[...]
    lane = lax.broadcasted_iota(jnp.int32, gc.shape, 1)
    lg_col = jnp.where(lane < M_HEADS,
                       jnp.where(valid_col, gc, NEG),
                       jnp.where(valid_col, _log_sigmoid(gc), 0.0))
    gr = grow_ref[...] + brow_ref[:, 0:1]
    valid_row = (c * t + lax.broadcasted_iota(jnp.int32, (1, t), 1)) >= N_PAD
    sub = lax.broadcasted_iota(jnp.int32, gr.shape, 0)
    lg_row = jnp.where(sub < M_HEADS,
                       jnp.where(valid_row, gr, NEG),
                       jnp.where(valid_row, _log_sigmoid(gr), 0.0))
    r_idx = lax.broadcasted_iota(jnp.int32, (t, t), 0)
    c_idx = lax.broadcasted_iota(jnp.int32, (t, t), 1)
    causal = c_idx <= r_idx
    tril = jnp.where(causal, 1.0, 0.0).astype(F32)
    triu = jnp.where(r_idx <= c_idx, 1.0, 0.0).astype(F32)
    cum_col = jnp.dot(tril, lg_col, preferred_element_type=F32, precision=lax.Precision.HIGHEST)
    cum_row = jnp.dot(lg_row, triu, preferred_element_type=F32, precision=lax.Precision.HIGHEST)

    nt_dims = (((1,), (1,)), ((), ()))
    tn_dims = (((0,), (0,)), ((), ()))
    heads = range(M_HEADS)
    sls = [slice(h * M_HEAD_DIM, (h + 1) * M_HEAD_DIM) for h in heads]

    q, k, qb, kb, vb = [], [], [], [], []
    for h in heads:
        xch = xc[:, sls[h]].astype(BF16)
        q.append(jnp.dot(xch, wq_ref[h], preferred_element_type=F32))
        k.append(jnp.dot(xch, wk_ref[h], preferred_element_type=F32))
        v = jnp.dot(xm[:, sls[h]].astype(BF16), wv_ref[h], preferred_element_type=F32)
        qb.append(q[h].astype(BF16))
        kb.append(k[h].astype(BF16))
        vb.append(v.astype(BF16))

    b_c = [cum_col[:, M_HEADS + h:M_HEADS + h + 1] for h in heads]
    li_c = [lg_col[:, h:h + 1] for h in heads]
    m_prev = [m_ref[h][:, 0:1] for h in heads]
    g, m_t, w_intra, scale_inter = [], [], [], []
    for h in heads:
        li_r = lg_row[h:h + 1, :]
        b_r = cum_row[M_HEADS + h:M_HEADS + h + 1, :]
        g.append(b_r[:, t - 1:t])
        d_mat = jnp.where(causal, b_c[h] - b_r + li_r, NEG)
        inter = b_c[h] + m_prev[h]
        m_t.append(jnp.maximum(inter, jnp.max(d_mat, axis=-1, keepdims=True)))
        s = lax.dot_general(qb[h], kb[h], nt_dims, preferred_element_type=F32)
        w_intra.append(s * jnp.exp(d_mat - m_t[h]))
        scale_inter.append(jnp.exp(inter - m_t[h]))

    hh = []
    for h in heads:
        num = (jnp.dot(w_intra[h].astype(BF16), vb[h], preferred_element_type=F32)
               + scale_inter[h] * jnp.dot(qb[h], c_ref[h].astype(BF16),
                                          preferred_element_type=F32))
        den = (jnp.sum(w_intra[h], axis=-1, keepdims=True)
               + scale_inter[h] * jnp.sum(q[h] * n_ref[h], axis=-1, keepdims=True))
        hh.append(num / jnp.maximum(jnp.abs(den), jnp.exp(-m_t[h])))

    for h in heads:
        a = g[h] - b_c[h] + li_c[h]
        m_new = jnp.maximum(g[h] + m_prev[h], jnp.max(a, axis=0, keepdims=True))
        w_state = jnp.exp(a - m_new)
        decay = jnp.exp(g[h] + m_prev[h] - m_new)
        kw = k[h] * w_state
        c_ref[h] = decay * c_ref[h] + lax.dot_general(kw.astype(BF16), vb[h], tn_dims,
                                                      preferred_element_type=F32)
        n_ref[h] = decay * n_ref[h] + jnp.sum(kw, axis=0, keepdims=True)
        m_ref[h] = jnp.broadcast_to(m_new, (1, LANES))

    for h in heads:
        gated = _sigmoid(opre_ref[:, sls[h]].astype(F32)) * hh[h]
        mu = jnp.mean(gated, axis=-1, keepdims=True)
        hc = gated - mu
        var = jnp.mean(hc * hc, axis=-1, keepdims=True)
        y_ref[:, sls[h]] = (hc * lax.rsqrt(var + LN_EPS) * ng_ref[:, sls[h]]).astype(BF16)


def _mlstm_call(p, gif, gif_t, conv_w, conv_b, wq, wk, wv, b_i, b_f, norm_g):
    n = p.shape[0]
    t = M_CHUNK
    halo = BF16_SUBLANES
    bias = jnp.concatenate([b_i, b_f]).astype(F32)
    bias_col = jnp.zeros((1, LANES), F32).at[0, :2 * M_HEADS].set(bias)
    bias_row = jnp.broadcast_to(bias[:, None], (2 * M_HEADS, LANES))
    const = lambda *shape: pl.BlockSpec(shape, lambda c: (0,) * len(shape))
    return pl.pallas_call(
        _mlstm_kernel,
        out_shape=jax.ShapeDtypeStruct((n, D_MODEL), BF16),
        grid=(n // t,),
        in_specs=[
            pl.BlockSpec((t, D_MODEL), lambda c: (c, COL_XM)),
            pl.BlockSpec((halo, D_MODEL),
                         lambda c: (jnp.maximum(c * (t // halo) - 1, 0), COL_XM)),
            pl.BlockSpec((t, D_MODEL), lambda c: (c, COL_OPRE)),
            pl.BlockSpec((t, LANES), lambda c: (c, 0)),
            pl.BlockSpec((2 * M_HEADS, t), lambda c: (0, c)),
            const(CONV_W, D_MODEL),
            const(1, D_MODEL),
            const(M_HEADS, M_HEAD_DIM, M_HEAD_DIM),
            const(M_HEADS, M_HEAD_DIM, M_HEAD_DIM),
            const(M_HEADS, M_HEAD_DIM, M_HEAD_DIM),
            const(1, LANES),
            const(2 * M_HEADS, LANES),
            const(1, D_MODEL),
        ],
        out_specs=pl.BlockSpec((t, D_MODEL), lambda c: (c, 0)),
        scratch_shapes=[
            pltpu.VMEM((t + halo, D_MODEL), F32),
            pltpu.VMEM((M_HEADS, M_HEAD_DIM, M_HEAD_DIM), F32),
            pltpu.VMEM((M_HEADS, 1, M_HEAD_DIM), F32),
            pltpu.VMEM((M_HEADS, 1, LANES), F32),
        ],
        compiler_params=_params(("arbitrary",)),
        name="mlstm",
    )(p, p, p, gif, gif_t, conv_w, conv_b.reshape(1, -1),
      wq.astype(BF16), (wk * M_HEAD_DIM ** -0.5).astype(BF16), wv.astype(BF16),
      bias_col, bias_row,
      norm_g.reshape(1, -1))


def _attn_kernel(qt_ref, k_ref, vt_ref, lq1_ref, lk1_ref, lq2_ref, lk2_ref, g_ref,
                 o_ref, qz_ref, s0_ref, s1_ref, mt0_ref, mt1_ref, m_ref, acc_ref,
                 *, lam_init):
    i = pl.program_id(1)
    tq, tk = ATTN_TQ, ATTN_TK
    s_refs = (s0_ref, s1_ref)
    mt_refs = (mt0_ref, mt1_ref)

    for r in range(tq // tk):
        qt = qt_ref[r]
        sub = lax.broadcasted_iota(jnp.int32, qt.shape, 0)
        zero = jnp.zeros_like(qt)
        qz_ref[:, r * tk:(r + 1) * tk] = jnp.where(sub < A_HEAD_DIM, qt, zero)
        qz_ref[:, tq + r * tk:tq + (r + 1) * tk] = jnp.where(sub >= A_HEAD_DIM, qt, zero)
    m_ref[...] = jnp.full_like(m_ref, NEG)
    acc_ref[...] = jnp.zeros_like(acc_ref)

    n_blocks = 2 * tq // ATTN_LANE_BLOCK
    lane_blocks = [slice(b * ATTN_LANE_BLOCK, (b + 1) * ATTN_LANE_BLOCK) for b in range(n_blocks)]

    n_diag = tq // tk
    LEAD_TILE = "lead"
    NO_TILE = "no tile"

    def block_state(kind, b):
        if kind is None:
            return "open"
        if kind == LEAD_TILE:
            return "lead"
        q_lo = (b * ATTN_LANE_BLOCK) % tq
        if q_lo + ATTN_LANE_BLOCK <= kind * tk:
            return "skip"
        if q_lo >= (kind + 1) * tk - 1:
            return "open"
        return "causal"

    def scores_block(j, buf, kind, b):
        state = block_state(kind, b)
        if state == "skip":
            return
        lanes = lane_blocks[b]
        start = pl.multiple_of(j * tk, tk)
        kt = k_ref[pl.ds(start, tk), :]
        s = jnp.dot(kt, qz_ref[:, lanes], preferred_element_type=F32)
        if state == "lead":
            kpos = j * tk + lax.broadcasted_iota(jnp.int32, s.shape, 0)
            col = b * ATTN_LANE_BLOCK + lax.broadcasted_iota(jnp.int32, s.shape, 1)
            qpos = i * tq + jnp.where(col >= tq, col - tq, col)
            s = jnp.where(kpos <= qpos, jnp.where(kpos >= N_PAD, s, NEG), NEG)
        elif state == "causal":
            k_rel = kind * tk + lax.broadcasted_iota(jnp.int32, s.shape, 0)
            q_rel = ((b * ATTN_LANE_BLOCK) % tq
                     + lax.broadcasted_iota(jnp.int32, s.shape, 1))
            s = jnp.where(k_rel <= q_rel, s, NEG)
        s_refs[buf][:, lanes] = s
        mt_refs[buf][:, lanes] = jnp.max(s, axis=0, keepdims=True)

    ones_rows = jnp.ones((BF16_SUBLANES, tk), BF16)

    def accumulate_block(j, buf, kind, b):
        if block_state(kind, b) == "skip":
            return
        lanes = lane_blocks[b]
        m_prev = m_ref[:, lanes]
        m_new = jnp.maximum(m_prev, mt_refs[buf][:, lanes])
        alpha = jnp.exp2(m_prev - m_new)
        p = jnp.exp2((s_refs[buf][:, lanes] - m_new).astype(BF16))
        m_ref[:, lanes] = m_new
        vt = jnp.concatenate([vt_ref[j], ones_rows], axis=0)
        pv = jnp.dot(vt, p, preferred_element_type=F32)
        acc_ref[:, lanes] = alpha * acc_ref[:, lanes] + pv

    def stage(j, kind, kind_next):
        for buf in (0, 1):
            @pl.when(j % 2 == buf)
            def _():
                for b in range(n_blocks):
                    accumulate_block(j, buf, kind, b)
                    if kind_next != NO_TILE:
                        scores_block(j + 1, 1 - buf, kind_next, b)

    n_k = (i + 1) * n_diag
    first_diag = n_k - n_diag
    for b in range(n_blocks):
        scores_block(0, 0, LEAD_TILE, b)

    def body(j, carry):
        stage(j, None, None)
        return carry

    lax.fori_loop(0, first_diag - 1, body, 0)

    @pl.when(first_diag > 0)
    def _():
        stage(first_diag - 1, None, 0)

    for d in range(1, n_diag):
        stage(first_diag - 1 + d, d - 1, d)
    stage(n_k - 1, n_diag - 1, NO_TILE)

    lam = (jnp.exp(jnp.sum(lq1_ref[...] * lk1_ref[...], axis=-1, keepdims=True))
           - jnp.exp(jnp.sum(lq2_ref[...] * lk2_ref[...], axis=-1, keepdims=True))
           + lam_init)
    out = acc_ref[0:A_VDIM, :] / acc_ref[A_VDIM:A_VDIM + 1, :]
    o = out[:, 0:tq] - lam * out[:, tq:2 * tq]
    o = o * lax.rsqrt(jnp.mean(o * o, axis=0, keepdims=True) + LN_EPS)
    o = o * g_ref[...] * (1.0 - lam_init)
    o_ref[...] = o.T.astype(BF16)


def _attn_call(p, q_t, v_t, lq1, lk1, lq2, lk2, norm_g, lam_init):
    n = p.shape[0]
    tq, tk = ATTN_TQ, ATTN_TK
    hpb = D_MODEL // A_VDIM
    vec = lambda d: pl.BlockSpec((1, d), lambda h, i: (0, 0))
    return pl.pallas_call(
        functools.partial(_attn_kernel, lam_init=lam_init),
        out_shape=jax.ShapeDtypeStruct((n, D_MODEL), BF16),
        grid=(A_HEADS, n // tq),
        in_specs=[
            pl.BlockSpec((tq // tk, A_VDIM, tk), lambda h, i: (i, h, 0)),
            pl.BlockSpec((n, A_VDIM), lambda h, i: (0, COL_K * hpb + h)),
            pl.BlockSpec((n // tk, A_VDIM, tk), lambda h, i: (0, h, 0)),
            vec(A_HEAD_DIM), vec(A_HEAD_DIM), vec(A_HEAD_DIM), vec(A_HEAD_DIM),
            pl.BlockSpec((A_VDIM, 1), lambda h, i: (0, 0)),
        ],
        out_specs=pl.BlockSpec((tq, A_VDIM), lambda h, i: (i, h)),
        scratch_shapes=[
            pltpu.VMEM((A_VDIM, 2 * tq), BF16),
            pltpu.VMEM((tk, 2 * tq), F32),
            pltpu.VMEM((tk, 2 * tq), F32),
            pltpu.VMEM((1, 2 * tq), F32),
            pltpu.VMEM((1, 2 * tq), F32),
            pltpu.VMEM((1, 2 * tq), F32),
            pltpu.VMEM((A_VDIM + BF16_SUBLANES, 2 * tq), F32),
        ],
        compiler_params=_params(("parallel", "arbitrary")),
        name="diff_attn",
    )(q_t, p, v_t, lq1.reshape(1, -1), lk1.reshape(1, -1), lq2.reshape(1, -1),
      lk2.reshape(1, -1), norm_g.reshape(-1, 1))


def _merge_kernel(ym_ref, ya_ref, ga_ref, gb_ref, h_ref, wm_ref, wa_ref, wo_ref,
                  g_ref, b_ref, wr_ref, rb_ref, h1_ref, comb_ref, *, alpha):
    bm = jnp.dot(ym_ref[...], wm_ref[...], preferred_element_type=F32)
    ba = jnp.dot(ya_ref[...], wa_ref[...], preferred_element_type=F32)
    merged = (_sigmoid(ga_ref[...].astype(F32)) * bm
              + _sigmoid(gb_ref[...].astype(F32)) * ba)
    proj = jnp.dot(merged.astype(BF16), wo_ref[...], preferred_element_type=F32)
    h1 = _layer_norm(alpha * h_ref[...] + proj, g_ref[...], b_ref[...])
    h1_ref[...] = h1

    h_hi = h1.astype(BF16)
    h_lo = (h1 - h_hi.astype(F32)).astype(BF16)
    r_hi = jnp.dot(h_hi, wr_ref[...], preferred_element_type=F32)
    r_lo = jnp.dot(h_lo, wr_ref[:, 0:LANES], preferred_element_type=F32)
    logits = (r_hi[:, 0:LANES] + r_lo) + r_hi[:, LANES:2 * LANES]
    lane = lax.broadcasted_iota(jnp.int32, logits.shape, 1)
    is_expert = lane < N_EXPERTS
    logits = jnp.where(is_expert, logits, NEG)
    ex = jnp.exp(logits - jnp.max(logits, axis=-1, keepdims=True))
    probs = ex / jnp.sum(ex, axis=-1, keepdims=True)
    sel = jnp.where(is_expert, probs + rb_ref[...], NEG)
    group = lane // EXPERTS_PER_GROUP
    best = jnp.max(jnp.where(group == 0, sel, NEG), axis=-1, keepdims=True)
    top_group = jnp.zeros_like(best, dtype=jnp.int32)
    for gidx in range(1, N_GROUPS):
        score = jnp.max(jnp.where(group == gidx, sel, NEG), axis=-1, keepdims=True)
        top_group = jnp.where(score > best, gidx, top_group)
        best = jnp.maximum(best, score)
    cand = jnp.where(group == top_group, sel, NEG)
    v1 = jnp.max(cand, axis=-1, keepdims=True)
    i1 = jnp.min(jnp.where(cand == v1, lane, LANES), axis=-1, keepdims=True)
    cand2 = jnp.where(lane == i1, NEG, cand)
    v2 = jnp.max(cand2, axis=-1, keepdims=True)
    i2 = jnp.min(jnp.where(cand2 == v2, lane, LANES), axis=-1, keepdims=True)
    p1 = jnp.sum(jnp.where(lane == i1, probs, 0.0), axis=-1, keepdims=True)
    p2 = jnp.sum(jnp.where(lane == i2, probs, 0.0), axis=-1, keepdims=True)
    tot = p1 + p2
    comb_ref[...] = (jnp.where(lane == i1, p1 / tot, 0.0)
                     + jnp.where(lane == i2, p2 / tot, 0.0))


def _merge_call(y_m, y_a, p, h, w_m, w_a, w_o, ln_g, ln_b, w_router, router_bias, alpha):
    n = h.shape[0]
    tm = MERGE_TILE
    full = pl.BlockSpec((D_MODEL, D_MODEL), lambda i: (0, 0))
    row = pl.BlockSpec((1, D_MODEL), lambda i: (0, 0))
    tile = lambda col: pl.BlockSpec((tm, D_MODEL), lambda i: (i, col))
    wr32 = jnp.zeros((D_MODEL, LANES), F32).at[:, :N_EXPERTS].set(w_router)
    wr_hi = wr32.astype(BF16)
    wr = jnp.concatenate([wr_hi, (wr32 - wr_hi.astype(F32)).astype(BF16)], axis=1)
    rb = jnp.zeros((1, LANES), F32).at[0, :N_EXPERTS].set(router_bias)
    return pl.pallas_call(
        functools.partial(_merge_kernel, alpha=alpha),
        out_shape=(jax.ShapeDtypeStruct((n, D_MODEL), F32),
                   jax.ShapeDtypeStruct((n, LANES), F32)),
        grid=(n // tm,),
        in_specs=[tile(0), tile(0), tile(COL_GA), tile(COL_GB), tile(0),
                  full, full, full, row, row,
                  pl.BlockSpec((D_MODEL, 2 * LANES), lambda i: (0, 0)),
                  pl.BlockSpec((1, LANES), lambda i: (0, 0))],
        out_specs=(tile(0), pl.BlockSpec((tm, LANES), lambda i: (i, 0))),
        compiler_params=_params(("parallel",)),
        name="merge_router",
    )(y_m, y_a, p, p, h, w_m, w_a, w_o, ln_g.reshape(1, -1), ln_b.reshape(1, -1), wr, rb)


def _moe_kernel(h_ref, combt_ref, wg_ref, wu_ref, wd_ref, g_ref, b_ref, o_ref,
                xb_ref, acc_ref, rankr_ref, *, alpha):
    e = pl.program_id(1)
    tm = MOE_TILE
    ns = MOE_SLOTS

    @pl.when(e == 0)
    def _():
        xb_ref[...] = h_ref[...].astype(BF16)
        acc_ref[...] = jnp.zeros_like(acc_ref)
        r_idx = lax.broadcasted_iota(jnp.int32, (tm, tm), 0)
        c_idx = lax.broadcasted_iota(jnp.int32, (tm, tm), 1)
        after = jnp.where(r_idx < c_idx, 1.0, 0.0).astype(BF16)
        rankr_ref[...] = jnp.dot(jnp.where(combt_ref[...] > 0.0, 1.0, 0.0).astype(BF16), after,
                                 preferred_element_type=F32)

    w_row = combt_ref[pl.ds(e, 1), :]
    routed_r = w_row > 0.0
    rank_r = rankr_ref[pl.ds(e, 1), :]
    count = jnp.sum(jnp.where(routed_r, 1.0, 0.0)).astype(jnp.int32)
    n_blocks = (count + ns - 1) // ns
    tn_dims = (((0,), (0,)), ((), ()))

    def block(b, carry):
        base = (b * ns).astype(F32)
        slot_r = lax.broadcasted_iota(jnp.int32, (ns, tm), 0).astype(F32) + base
        hit = routed_r & (rank_r == slot_r)
        sel = jnp.where(hit, 1.0, 0.0).astype(BF16)
        w_slot = jnp.sum(jnp.where(hit, w_row, 0.0), axis=-1, keepdims=True)
        xs = jnp.dot(sel, xb_ref[...], preferred_element_type=F32).astype(BF16)
        gate = jnp.dot(xs, wg_ref[0], preferred_element_type=F32)
        up = jnp.dot(xs, wu_ref[0], preferred_element_type=F32)
        mid = (gate * _sigmoid(gate) * up).astype(BF16)
        out = jnp.dot(mid, wd_ref[0], preferred_element_type=F32) * w_slot
        acc_ref[...] += lax.dot_general(sel, out.astype(BF16), tn_dims,
                                        preferred_element_type=F32)
        return carry

    lax.fori_loop(0, n_blocks, block, 0)

    @pl.when(e == N_EXPERTS - 1)
    def _():
        o_ref[...] = _layer_norm(alpha * h_ref[...] + acc_ref[...], g_ref[...], b_ref[...])


def _moe_call(h1, comb, w_gate, w_up, w_down, ln_g, ln_b, alpha):
    n = h1.shape[0]
    tm = MOE_TILE
    comb_t = comb[:, :N_EXPERTS].T
    wspec = pl.BlockSpec((1, D_MODEL, D_MODEL), lambda i, e: (e, 0, 0))
    row = pl.BlockSpec((1, D_MODEL), lambda i, e: (0, 0))
    return pl.pallas_call(
        functools.partial(_moe_kernel, alpha=alpha),
        out_shape=jax.ShapeDtypeStruct((n, D_MODEL), F32),
        grid=(n // tm, N_EXPERTS),
        in_specs=[pl.BlockSpec((tm, D_MODEL), lambda i, e: (i, 0),
                               pipeline_mode=pl.Buffered(1)),
                  pl.BlockSpec((N_EXPERTS, tm), lambda i, e: (0, i)),
                  wspec, wspec, wspec, row, row],
        out_specs=pl.BlockSpec((tm, D_MODEL), lambda i, e: (i, 0)),
        scratch_shapes=[pltpu.VMEM((tm, D_MODEL), BF16),
                        pltpu.VMEM((tm, D_MODEL), F32),
                        pltpu.VMEM((N_EXPERTS, tm), F32)],
        compiler_params=_params(("parallel", "arbitrary")),
        name="moe",
    )(h1, comb_t, w_gate, w_up, w_down, ln_g.reshape(1, -1), ln_b.reshape(1, -1))


def kernel(x, meta_tokens, ln_in_g, ln_in_b, w_in, conv_w, conv_b, wq_m, wk_m, wv_m, b_i, b_f, m_norm_g, lam_q1, lam_k1, lam_q2, lam_k2, a_norm_g, w_branch_m, w_branch_a, w_out, ln1_g, ln1_b, ln2_g, ln2_b, w_router, router_bias, w_gate_e, w_up_e, w_down_e):
    bsz, seq, dm = x.shape
    assert bsz == 1 and dm == D_MODEL
    depth = w_in.shape[0]
    alpha = float((2 * depth) ** 0.25)
    length = LEAD + seq
    n = -(-length // ROW_ALIGN) * ROW_ALIGN

    assert seq % LEAD == 0
    lead = jnp.concatenate([jnp.zeros((N_PAD, dm), F32), meta_tokens.astype(F32)], axis=0)
    h = _ln_call(x[0].astype(F32), lead, ln_in_g, ln_in_b, n)

    o_if = 2 * D_MODEL
    o_rest = o_if + 2 * M_HEADS
    for l in range(depth):
        w = w_in[l]
        q_scale = A_HEAD_DIM ** -0.5 * math.log2(math.e)
        blk = lambda b: w[:, o_rest + b * D_MODEL:o_rest + (b + 1) * D_MODEL]
        w_q, w_k, w_v, w_ga, w_gb = (blk(b) for b in range(5))
        w_main = jnp.concatenate([w[:, :o_if], w_k, w_ga, w_gb, w_q * q_scale, w_v],
                                 axis=1).astype(BF16)
        w_if = jnp.zeros((dm, LANES), F32).at[:, :2 * M_HEADS].set(w[:, o_if:o_rest]).astype(BF16)
        p, gif, q_t, v_t = _inproj_call(h, w_main, w_if)
        gif_t = gif[:, :2 * M_HEADS].T
        y_m = _mlstm_call(p, gif, gif_t, conv_w[l], conv_b[l], wq_m[l], wk_m[l], wv_m[l],
                          b_i[l], b_f[l], m_norm_g[l])
        lam_init = 0.8 - 0.6 * math.exp(-0.3 * l)
        y_a = _attn_call(p, q_t, v_t, lam_q1[l], lam_k1[l], lam_q2[l], lam_k2[l], a_norm_g[l],
                         lam_init)
        h1, comb = _merge_call(y_m, y_a, p, h, w_branch_m[l].astype(BF16),
                               w_branch_a[l].astype(BF16), w_out[l].astype(BF16),
                               ln1_g[l], ln1_b[l], w_router, router_bias, alpha)
        h = _moe_call(h1, comb, w_gate_e[l].astype(BF16), w_up_e[l].astype(BF16),
                      w_down_e[l].astype(BF16), ln2_g[l], ln2_b[l], alpha)
    return h[LEAD:length][None].astype(x.dtype)
```

```python
import functools
import math

import jax
import jax.numpy as jnp
from jax import lax
from jax.experimental import pallas as pl
from jax.experimental.pallas import tpu as pltpu

F32 = jnp.float32
BF16 = jnp.bfloat16

D_MODEL = 1024
N_META = 16
LEAD = 128
N_PAD = LEAD - N_META
M_HEADS = 4
M_HEAD_DIM = D_MODEL // M_HEADS
CONV_W = 4
A_HEADS = 8
A_HEAD_DIM = D_MODEL // (2 * A_HEADS)
A_VDIM = 2 * A_HEAD_DIM
N_EXPERTS = 16
N_GROUPS = 4
EXPERTS_PER_GROUP = N_EXPERTS // N_GROUPS
LN_EPS = 1e-5
NEG = -1e30

LANES = 128
BF16_SUBLANES = 16
V7X_VMEM_BYTES = 64 * 1024 * 1024
VMEM_LIMIT_BYTES = V7X_VMEM_BYTES * 7 // 8

LN_TILE = 512
LN_PIECES = LN_TILE // LEAD
PROJ_TILE_M = 1536
PROJ_TILE_N = 1024
M_CHUNK = 256
ATTN_TQ = 1536
ATTN_TK = 768
ATTN_LANE_BLOCK = 256
MERGE_TILE = 768
MOE_TILE = 1536
MOE_SLOTS = 256
ROW_ALIGN = math.lcm(LN_TILE, PROJ_TILE_M, M_CHUNK, ATTN_TQ, ATTN_TK, MERGE_TILE, MOE_TILE)

COL_XM, COL_OPRE, COL_K, COL_GA, COL_GB = range(5)
N_ROW_BLOCKS = 5


def _params(semantics):
    return pltpu.CompilerParams(dimension_semantics=semantics,
                                vmem_limit_bytes=VMEM_LIMIT_BYTES)


def _layer_norm(x, g, b):
    mu = jnp.mean(x, axis=-1, keepdims=True)
    xc = x - mu
    var = jnp.mean(xc * xc, axis=-1, keepdims=True)
    return xc * lax.rsqrt(var + LN_EPS) * g + b


def _sigmoid(x):
    return 1.0 / (1.0 + jnp.exp(-x))


def _ln_kernel(*refs, n_x_blocks):
    x_refs, (lead_ref, g_ref, b_ref, o_ref) = refs[:LN_PIECES], refs[LN_PIECES:]
    i = pl.program_id(0)
    for k, x_ref in enumerate(x_refs):
        blk = LN_PIECES * i - 1 + k
        rows = jnp.where(blk < 0, lead_ref[...], x_ref[...])
        rows = jnp.where(blk >= n_x_blocks, 0.0, rows)
        o_ref[k * LEAD:(k + 1) * LEAD, :] = _layer_norm(rows, g_ref[...], b_ref[...])


def _ln_call(x, lead, g, b, n):
    n_x_blocks = x.shape[0] // LEAD
    row = pl.BlockSpec((1, D_MODEL), lambda i: (0, 0))
    piece = lambda k: pl.BlockSpec(
        (LEAD, D_MODEL), lambda i: (jnp.clip(LN_PIECES * i - 1 + k, 0, n_x_blocks - 1), 0))
    return pl.pallas_call(
        functools.partial(_ln_kernel, n_x_blocks=n_x_blocks),
        out_shape=jax.ShapeDtypeStruct((n, D_MODEL), F32),
        grid=(n // LN_TILE,),
        in_specs=[piece(k) for k in range(LN_PIECES)]
        + [pl.BlockSpec((LEAD, D_MODEL), lambda i: (0, 0)), row, row],
        out_specs=pl.BlockSpec((LN_TILE, D_MODEL), lambda i: (i, 0)),
        compiler_params=_params(("parallel",)),
        name="ln_in",
    )(*([x] * LN_PIECES), lead, g.reshape(1, -1), b.reshape(1, -1))


def _inproj_kernel(h_ref, w_ref, wif_ref, p_ref, gif_ref, qt_ref, vt_ref, hb_ref):
    j = pl.program_id(1)
    tk = ATTN_TK

    @pl.when(j == 0)
    def _():
        hb_ref[...] = h_ref[...].astype(BF16)
        gif_ref[...] = jnp.dot(hb_ref[...], wif_ref[...], preferred_element_type=F32)

    def project():
        return jnp.dot(hb_ref[...], w_ref[...], preferred_element_type=F32)

    @pl.when(j < N_ROW_BLOCKS)
    def _():
        p_ref[...] = project().astype(BF16)

    def put_transposed(o_ref):
        res = project()
        for r in range(PROJ_TILE_M // tk):
            o_ref[r] = res[r * tk:(r + 1) * tk, :].T.astype(BF16)

    @pl.when(j == N_ROW_BLOCKS)
    def _():
        put_transposed(qt_ref)

    @pl.when(j == N_ROW_BLOCKS + 1)
    def _():
        put_transposed(vt_ref)


def _inproj_call(h, w_main, w_if):
    n = h.shape[0]
    tm, tk = PROJ_TILE_M, ATTN_TK
    t_shape = jax.ShapeDtypeStruct((n // tk, D_MODEL, tk), BF16)
    t_spec = pl.BlockSpec((tm // tk, D_MODEL, tk), lambda i, j: (i, 0, 0))
    return pl.pallas_call(
        _inproj_kernel,
        out_shape=(jax.ShapeDtypeStruct((n, N_ROW_BLOCKS * D_MODEL), BF16),
                   jax.ShapeDtypeStruct((n, LANES), F32), t_shape, t_shape),
        grid=(n // tm, w_main.shape[1] // PROJ_TILE_N),
        in_specs=[pl.BlockSpec((tm, D_MODEL), lambda i, j: (i, 0)),
                  pl.BlockSpec((D_MODEL, PROJ_TILE_N), lambda i, j: (0, j)),
                  pl.BlockSpec((D_MODEL, LANES), lambda i, j: (0, 0))],
        out_specs=(pl.BlockSpec((tm, PROJ_TILE_N),
                                lambda i, j: (i, jnp.minimum(j, N_ROW_BLOCKS - 1))),
                   pl.BlockSpec((tm, LANES), lambda i, j: (i, 0)), t_spec, t_spec),
        scratch_shapes=[pltpu.VMEM((tm, D_MODEL), BF16)],
        compiler_params=_params(("parallel", "arbitrary")),
        name="in_proj",
    )(h, w_main, w_if)


def _log_sigmoid(x):
    return jnp.minimum(x, 0.0) - jnp.log1p(jnp.exp(-jnp.abs(x)))


def _mlstm_kernel(xm_ref, halo_ref, opre_ref, gcol_ref, grow_ref, convw_ref, convb_ref,
                  wq_ref, wk_ref, wv_ref, bcol_ref, brow_ref, ng_ref,
                  y_ref, xs_ref, c_ref, n_ref, m_ref):
    c = pl.program_id(0)
    t = M_CHUNK
    halo = BF16_SUBLANES

    @pl.when(c == 0)
    def _():
        c_ref[...] = jnp.zeros_like(c_ref)
        n_ref[...] = jnp.zeros_like(n_ref)
        m_ref[...] = jnp.zeros_like(m_ref)

    pos_col = c * t + lax.broadcasted_iota(jnp.int32, (t, 1), 0)
    valid_col = pos_col >= N_PAD
    xm = jnp.where(valid_col, xm_ref[...].astype(F32), 0.0)
    halo_pos = c * t - halo + lax.broadcasted_iota(jnp.int32, (halo, 1), 0)
    xs_ref[0:halo, :] = jnp.where(halo_pos >= N_PAD, halo_ref[...].astype(F32), 0.0)
    xs_ref[halo:halo + t, :] = xm
    conv = xm * convw_ref[CONV_W - 1:CONV_W, :] + convb_ref[...]
    for k in range(CONV_W - 1):
        off = halo - (CONV_W - 1) + k
        conv = conv + xs_ref[off:off + t, :] * convw_ref[k:k + 1, :]
    xc = conv * _sigmoid(conv)

    gc = gcol_ref[...] + bcol_ref[...]
    lane = lax.broadcasted_iota(jnp.int32, gc.shape, 1)
    lg_col = jnp.where(lane < M_HEADS,
                       jnp.where(valid_col, gc, NEG),
                       jnp.where(valid_col, _log_sigmoid(gc), 0.0))
    gr = grow_ref[...] + brow_ref[:, 0:1]
    valid_row = (c * t + lax.broadcasted_iota(jnp.int32, (1, t), 1)) >= N_PAD
    sub = lax.broadcasted_iota(jnp.int32, gr.shape, 0)
    lg_row = jnp.where(sub < M_HEADS,
                       jnp.where(valid_row, gr, NEG),
                       jnp.where(valid_row, _log_sigmoid(gr), 0.0))
    r_idx = lax.broadcasted_iota(jnp.int32, (t, t), 0)
    c_idx = lax.broadcasted_iota(jnp.int32, (t, t), 1)
    causal = c_idx <= r_idx
    tril = jnp.where(causal, 1.0, 0.0).astype(F32)
    triu = jnp.where(r_idx <= c_idx, 1.0, 0.0).astype(F32)
    cum_col = jnp.dot(tril, lg_col, preferred_element_type=F32, precision=lax.Precision.HIGHEST)
    cum_row = jnp.dot(lg_row, triu, preferred_element_type=F32, precision=lax.Precision.HIGHEST)

    nt_dims = (((1,), (1,)), ((), ()))
    tn_dims = (((0,), (0,)), ((), ()))
    heads = range(M_HEADS)
    sls = [slice(h * M_HEAD_DIM, (h + 1) * M_HEAD_DIM) for h in heads]

    q, k, qb, kb, vb = [], [], [], [], []
    for h in heads:
        xch = xc[:, sls[h]].astype(BF16)
        q.append(jnp.dot(xch, wq_ref[h], preferred_element_type=F32))
        k.append(jnp.dot(xch, wk_ref[h], preferred_element_type=F32))
        v = jnp.dot(xm[:, sls[h]].astype(BF16), wv_ref[h], preferred_element_type=F32)
        qb.append(q[h].astype(BF16))
        kb.append(k[h].astype(BF16))
        vb.append(v.astype(BF16))

    b_c = [cum_col[:, M_HEADS + h:M_HEADS + h + 1] for h in heads]
    li_c = [lg_col[:, h:h + 1] for h in heads]
    m_prev = [m_ref[h][:, 0:1] for h in heads]
    g, m_t, w_intra, scale_inter = [], [], [], []
    for h in heads:
        li_r = lg_row[h:h + 1, :]
        b_r = cum_row[M_HEADS + h:M_HEADS + h + 1, :]
        g.append(b_r[:, t - 1:t])
        d_mat = jnp.where(causal, b_c[h] - b_r + li_r, NEG)
        inter = b_c[h] + m_prev[h]
        m_t.append(jnp.maximum(inter, jnp.max(d_mat, axis=-1, keepdims=True)))
        s = lax.dot_general(qb[h], kb[h], nt_dims, preferred_element_type=F32)
        w_intra.append(s * jnp.exp(d_mat - m_t[h]))
        scale_inter.append(jnp.exp(inter - m_t[h]))

    hh = []
    for h in heads:
        num = (jnp.dot(w_intra[h].astype(BF16), vb[h], preferred_element_type=F32)
               + scale_inter[h] * jnp.dot(qb[h], c_ref[h].astype(BF16),
                                          preferred_element_type=F32))
        den = (jnp.sum(w_intra[h], axis=-1, keepdims=True)
               + scale_inter[h] * jnp.sum(q[h] * n_ref[h], axis=-1, keepdims=True))
        hh.append(num / jnp.maximum(jnp.abs(den), jnp.exp(-m_t[h])))

    for h in heads:
        a = g[h] - b_c[h] + li_c[h]
        m_new = jnp.maximum(g[h] + m_prev[h], jnp.max(a, axis=0, keepdims=True))
        w_state = jnp.exp(a - m_new)
        decay = jnp.exp(g[h] + m_prev[h] - m_new)
        kw = k[h] * w_state
        c_ref[h] = decay * c_ref[h] + lax.dot_general(kw.astype(BF16), vb[h], tn_dims,
                                                      preferred_element_type=F32)
        n_ref[h] = decay * n_ref[h] + jnp.sum(kw, axis=0, keepdims=True)
        m_ref[h] = jnp.broadcast_to(m_new, (1, LANES))

    for h in heads:
        gated = _sigmoid(opre_ref[:, sls[h]].astype(F32)) * hh[h]
        mu = jnp.mean(gated, axis=-1, keepdims=True)
        hc = gated - mu
        var = jnp.mean(hc * hc, axis=-1, keepdims=True)
        y_ref[:, sls[h]] = (hc * lax.rsqrt(var + LN_EPS) * ng_ref[:, sls[h]]).astype(BF16)


def _mlstm_call(p, gif, gif_t, conv_w, conv_b, wq, wk, wv, b_i, b_f, norm_g):
    n = p.shape[0]
    t = M_CHUNK
    halo = BF16_SUBLANES
    bias = jnp.concatenate([b_i, b_f]).astype(F32)
    bias_col = jnp.zeros((1, LANES), F32).at[0, :2 * M_HEADS].set(bias)
    bias_row = jnp.broadcast_to(bias[:, None], (2 * M_HEADS, LANES))
    const = lambda *shape: pl.BlockSpec(shape, lambda c: (0,) * len(shape))
    return pl.pallas_call(
        _mlstm_kernel,
        out_shape=jax.ShapeDtypeStruct((n, D_MODEL), BF16),
        grid=(n // t,),
        in_specs=[
            pl.BlockSpec((t, D_MODEL), lambda c: (c, COL_XM)),
            pl.BlockSpec((halo, D_MODEL),
                         lambda c: (jnp.maximum(c * (t // halo) - 1, 0), COL_XM)),
            pl.BlockSpec((t, D_MODEL), lambda c: (c, COL_OPRE)),
            pl.BlockSpec((t, LANES), lambda c: (c, 0)),
            pl.BlockSpec((2 * M_HEADS, t), lambda c: (0, c)),
            const(CONV_W, D_MODEL),
            const(1, D_MODEL),
            const(M_HEADS, M_HEAD_DIM, M_HEAD_DIM),
            const(M_HEADS, M_HEAD_DIM, M_HEAD_DIM),
            const(M_HEADS, M_HEAD_DIM, M_HEAD_DIM),
            const(1, LANES),
            const(2 * M_HEADS, LANES),
            const(1, D_MODEL),
        ],
        out_specs=pl.BlockSpec((t, D_MODEL), lambda c: (c, 0)),
        scratch_shapes=[
            pltpu.VMEM((t + halo, D_MODEL), F32),
            pltpu.VMEM((M_HEADS, M_HEAD_DIM, M_HEAD_DIM), F32),
            pltpu.VMEM((M_HEADS, 1, M_HEAD_DIM), F32),
            pltpu.VMEM((M_HEADS, 1, LANES), F32),
        ],
        compiler_params=_params(("arbitrary",)),
        name="mlstm",
    )(p, p, p, gif, gif_t, conv_w, conv_b.reshape(1, -1),
      wq.astype(BF16), (wk * M_HEAD_DIM ** -0.5).astype(BF16), wv.astype(BF16),
      bias_col, bias_row,
      norm_g.reshape(1, -1))


def _attn_kernel(qt_ref, k_ref, vt_ref, lq1_ref, lk1_ref, lq2_ref, lk2_ref, g_ref,
                 o_ref, qz_ref, s0_ref, s1_ref, mt0_ref, mt1_ref, m_ref, acc_ref,
                 *, lam_init):
    i = pl.program_id(1)
    tq, tk = ATTN_TQ, ATTN_TK
    s_refs = (s0_ref, s1_ref)
    mt_refs = (mt0_ref, mt1_ref)

    for r in range(tq // tk):
        qt = qt_ref[r]
        sub = lax.broadcasted_iota(jnp.int32, qt.shape, 0)
        zero = jnp.zeros_like(qt)
        qz_ref[:, r * tk:(r + 1) * tk] = jnp.where(sub < A_HEAD_DIM, qt, zero)
        qz_ref[:, tq + r * tk:tq + (r + 1) * tk] = jnp.where(sub >= A_HEAD_DIM, qt, zero)
    m_ref[...] = jnp.full_like(m_ref, NEG)
    acc_ref[...] = jnp.zeros_like(acc_ref)

    n_blocks = 2 * tq // ATTN_LANE_BLOCK
    lane_blocks = [slice(b * ATTN_LANE_BLOCK, (b + 1) * ATTN_LANE_BLOCK) for b in range(n_blocks)]

    n_diag = tq // tk
    LEAD_TILE = "lead"
    NO_TILE = "no tile"

    def block_state(kind, b):
        if kind is None:
            return "open"
        if kind == LEAD_TILE:
            return "lead"
        q_lo = (b * ATTN_LANE_BLOCK) % tq
        if q_lo + ATTN_LANE_BLOCK <= kind * tk:
            return "skip"
        if q_lo >= (kind + 1) * tk - 1:
            return "open"
        return "causal"

    def scores_block(j, buf, kind, b):
        state = block_state(kind, b)
        if state == "skip":
            return
        lanes = lane_blocks[b]
        start = pl.multiple_of(j * tk, tk)
        kt = k_ref[pl.ds(start, tk), :]
        s = jnp.dot(kt, qz_ref[:, lanes], preferred_element_type=F32)
        if state == "lead":
            kpos = j * tk + lax.broadcasted_iota(jnp.int32, s.shape, 0)
            col = b * ATTN_LANE_BLOCK + lax.broadcasted_iota(jnp.int32, s.shape, 1)
            qpos = i * tq + jnp.where(col >= tq, col - tq, col)
            s = jnp.where(kpos <= qpos, jnp.where(kpos >= N_PAD, s, NEG), NEG)
        elif state == "causal":
            k_rel = kind * tk + lax.broadcasted_iota(jnp.int32, s.shape, 0)
            q_rel = ((b * ATTN_LANE_BLOCK) % tq
                     + lax.broadcasted_iota(jnp.int32, s.shape, 1))
            s = jnp.where(k_rel <= q_rel, s, NEG)
        s_refs[buf][:, lanes] = s
        mt_refs[buf][:, lanes] = jnp.max(s, axis=0, keepdims=True)

    ones_rows = jnp.ones((BF16_SUBLANES, tk), BF16)

    def accumulate_block(j, buf, kind, b):
        if block_state(kind, b) == "skip":
            return
        lanes = lane_blocks[b]
        m_prev = m_ref[:, lanes]
        m_new = jnp.maximum(m_prev, mt_refs[buf][:, lanes])
        alpha = jnp.exp2(m_prev - m_new)
        p = jnp.exp2((s_refs[buf][:, lanes] - m_new).astype(BF16))
        m_ref[:, lanes] = m_new
        vt = jnp.concatenate([vt_ref[j], ones_rows], axis=0)
        pv = jnp.dot(vt, p, preferred_element_type=F32)
        acc_ref[:, lanes] = alpha * acc_ref[:, lanes] + pv

    def stage(j, kind, kind_next):
        for buf in (0, 1):
            @pl.when(j % 2 == buf)
            def _():
                for b in range(n_blocks):
                    accumulate_block(j, buf, kind, b)
                    if kind_next != NO_TILE:
                        scores_block(j + 1, 1 - buf, kind_next, b)

    n_k = (i + 1) * n_diag
    first_diag = n_k - n_diag
    for b in range(n_blocks):
        scores_block(0, 0, LEAD_TILE, b)

    def body(j, carry):
        stage(j, None, None)
        return carry

    lax.fori_loop(0, first_diag - 1, body, 0)

    @pl.when(first_diag > 0)
    def _():
        stage(first_diag - 1, None, 0)

    for d in range(1, n_diag):
        stage(first_diag - 1 + d, d - 1, d)
    stage(n_k - 1, n_diag - 1, NO_TILE)

    lam = (jnp.exp(jnp.sum(lq1_ref[...] * lk1_ref[...], axis=-1, keepdims=True))
           - jnp.exp(jnp.sum(lq2_ref[...] * lk2_ref[...], axis=-1, keepdims=True))
           + lam_init)
    out = acc_ref[0:A_VDIM, :] / acc_ref[A_VDIM:A_VDIM + 1, :]
    o = out[:, 0:tq] - lam * out[:, tq:2 * tq]
    o = o * lax.rsqrt(jnp.mean(o * o, axis=0, keepdims=True) + LN_EPS)
    o = o * g_ref[...] * (1.0 - lam_init)
    o_ref[...] = o.T.astype(BF16)


def _attn_call(p, q_t, v_t, lq1, lk1, lq2, lk2, norm_g, lam_init):
    n = p.shape[0]
    tq, tk = ATTN_TQ, ATTN_TK
    hpb = D_MODEL // A_VDIM
    vec = lambda d: pl.BlockSpec((1, d), lambda h, i: (0, 0))
    return pl.pallas_call(
        functools.partial(_attn_kernel, lam_init=lam_init),
        out_shape=jax.ShapeDtypeStruct((n, D_MODEL), BF16),
        grid=(A_HEADS, n // tq),
        in_specs=[
            pl.BlockSpec((tq // tk, A_VDIM, tk), lambda h, i: (i, h, 0)),
            pl.BlockSpec((n, A_VDIM), lambda h, i: (0, COL_K * hpb + h)),
            pl.BlockSpec((n // tk, A_VDIM, tk), lambda h, i: (0, h, 0)),
            vec(A_HEAD_DIM), vec(A_HEAD_DIM), vec(A_HEAD_DIM), vec(A_HEAD_DIM),
            pl.BlockSpec((A_VDIM, 1), lambda h, i: (0, 0)),
        ],
        out_specs=pl.BlockSpec((tq, A_VDIM), lambda h, i: (i, h)),
        scratch_shapes=[
            pltpu.VMEM((A_VDIM, 2 * tq), BF16),
            pltpu.VMEM((tk, 2 * tq), F32),
            pltpu.VMEM((tk, 2 * tq), F32),
            pltpu.VMEM((1, 2 * tq), F32),
            pltpu.VMEM((1, 2 * tq), F32),
            pltpu.VMEM((1, 2 * tq), F32),
            pltpu.VMEM((A_VDIM + BF16_SUBLANES, 2 * tq), F32),
        ],
        compiler_params=_params(("parallel", "arbitrary")),
        name="diff_attn",
    )(q_t, p, v_t, lq1.reshape(1, -1), lk1.reshape(1, -1), lq2.reshape(1, -1),
      lk2.reshape(1, -1), norm_g.reshape(-1, 1))


def _merge_kernel(ym_ref, ya_ref, ga_ref, gb_ref, h_ref, wm_ref, wa_ref, wo_ref,
                  g_ref, b_ref, wr_ref, rb_ref, h1_ref, comb_ref, *, alpha):
    bm = jnp.dot(ym_ref[...], wm_ref[...], preferred_element_type=F32)
    ba = jnp.dot(ya_ref[...], wa_ref[...], preferred_element_type=F32)
    merged = (_sigmoid(ga_ref[...].astype(F32)) * bm
              + _sigmoid(gb_ref[...].astype(F32)) * ba)
    proj = jnp.dot(merged.astype(BF16), wo_ref[...], preferred_element_type=F32)
    h1 = _layer_norm(alpha * h_ref[...] + proj, g_ref[...], b_ref[...])
    h1_ref[...] = h1
    comb_ref[...] = _route(h1, wr_ref, rb_ref)


def _route(h1, wr_ref, rb_ref):
    h_hi = h1.astype(BF16)
    h_lo = (h1 - h_hi.astype(F32)).astype(BF16)
    r_hi = jnp.dot(h_hi, wr_ref[...], preferred_element_type=F32)
    r_lo = jnp.dot(h_lo, wr_ref[:, 0:LANES], preferred_element_type=F32)
    logits = (r_hi[:, 0:LANES] + r_lo) + r_hi[:, LANES:2 * LANES]
    lane = lax.broadcasted_iota(jnp.int32, logits.shape, 1)
    is_expert = lane < N_EXPERTS
    logits = jnp.where(is_expert, logits, NEG)
    ex = jnp.exp(logits - jnp.max(logits, axis=-1, keepdims=True))
    probs = ex / jnp.sum(ex, axis=-1, keepdims=True)
    sel = jnp.where(is_expert, probs + rb_ref[...], NEG)
    group = lane // EXPERTS_PER_GROUP
    best = jnp.max(jnp.where(group == 0, sel, NEG), axis=-1, keepdims=True)
    top_group = jnp.zeros_like(best, dtype=jnp.int32)
    for gidx in range(1, N_GROUPS):
        score = jnp.max(jnp.where(group == gidx, sel, NEG), axis=-1, keepdims=True)
        top_group = jnp.where(score > best, gidx, top_group)
        best = jnp.maximum(best, score)
    cand = jnp.where(group == top_group, sel, NEG)
    v1 = jnp.max(cand, axis=-1, keepdims=True)
    i1 = jnp.min(jnp.where(cand == v1, lane, LANES), axis=-1, keepdims=True)
    cand2 = jnp.where(lane == i1, NEG, cand)
    v2 = jnp.max(cand2, axis=-1, keepdims=True)
    i2 = jnp.min(jnp.where(cand2 == v2, lane, LANES), axis=-1, keepdims=True)
    p1 = jnp.sum(jnp.where(lane == i1, probs, 0.0), axis=-1, keepdims=True)
    p2 = jnp.sum(jnp.where(lane == i2, probs, 0.0), axis=-1, keepdims=True)
    tot = p1 + p2
    return jnp.where(lane == i1, p1 / tot, 0.0) + jnp.where(lane == i2, p2 / tot, 0.0)


def _merge_call(y_m, y_a, p, h, w_m, w_a, w_o, ln_g, ln_b, w_router, router_bias, alpha):
    n = h.shape[0]
    tm = MERGE_TILE
    full = pl.BlockSpec((D_MODEL, D_MODEL), lambda i: (0, 0))
    row = pl.BlockSpec((1, D_MODEL), lambda i: (0, 0))
    tile = lambda col: pl.BlockSpec((tm, D_MODEL), lambda i: (i, col))
    wr32 = jnp.zeros((D_MODEL, LANES), F32).at[:, :N_EXPERTS].set(w_router)
    wr_hi = wr32.astype(BF16)
    wr = jnp.concatenate([wr_hi, (wr32 - wr_hi.astype(F32)).astype(BF16)], axis=1)
    rb = jnp.zeros((1, LANES), F32).at[0, :N_EXPERTS].set(router_bias)
    return pl.pallas_call(
        functools.partial(_merge_kernel, alpha=alpha),
        out_shape=(jax.ShapeDtypeStruct((n, D_MODEL), F32),
                   jax.ShapeDtypeStruct((n, LANES), F32)),
        grid=(n // tm,),
        in_specs=[tile(0), tile(0), tile(COL_GA), tile(COL_GB), tile(0),
                  full, full, full, row, row,
                  pl.BlockSpec((D_MODEL, 2 * LANES), lambda i: (0, 0)),
                  pl.BlockSpec((1, LANES), lambda i: (0, 0))],
        out_specs=(tile(0), pl.BlockSpec((tm, LANES), lambda i: (i, 0))),
        compiler_params=_params(("parallel",)),
        name="merge_router",
    )(y_m, y_a, p, p, h, w_m, w_a, w_o, ln_g.reshape(1, -1), ln_b.reshape(1, -1), wr, rb)


def _moe_kernel(h_ref, combt_ref, wg_ref, wu_ref, wd_ref, g_ref, b_ref, o_ref,
                xb_ref, acc_ref, rankr_ref, *, alpha):
    e = pl.program_id(1)
    tm = MOE_TILE
    ns = MOE_SLOTS

    @pl.when(e == 0)
    def _():
        xb_ref[...] = h_ref[...].astype(BF16)
        acc_ref[...] = jnp.zeros_like(acc_ref)
        r_idx = lax.broadcasted_iota(jnp.int32, (tm, tm), 0)
        c_idx = lax.broadcasted_iota(jnp.int32, (tm, tm), 1)
        after = jnp.where(r_idx < c_idx, 1.0, 0.0).astype(BF16)
        rankr_ref[...] = jnp.dot(jnp.where(combt_ref[...] > 0.0, 1.0, 0.0).astype(BF16), after,
                                 preferred_element_type=F32)

    w_row = combt_ref[pl.ds(e, 1), :]
    routed_r = w_row > 0.0
    rank_r = rankr_ref[pl.ds(e, 1), :]
    count = jnp.sum(jnp.where(routed_r, 1.0, 0.0)).astype(jnp.int32)
    n_blocks = (count + ns - 1) // ns
    tn_dims = (((0,), (0,)), ((), ()))

    def block(b, carry):
        base = (b * ns).astype(F32)
        slot_r = lax.broadcasted_iota(jnp.int32, (ns, tm), 0).astype(F32) + base
        hit = routed_r & (rank_r == slot_r)
        sel = jnp.where(hit, 1.0, 0.0).astype(BF16)
        w_slot = jnp.sum(jnp.where(hit, w_row, 0.0), axis=-1, keepdims=True)
        xs = jnp.dot(sel, xb_ref[...], preferred_element_type=F32).astype(BF16)
        gate = jnp.dot(xs, wg_ref[0], preferred_element_type=F32)
        up = jnp.dot(xs, wu_ref[0], preferred_element_type=F32)
        mid = (gate * _sigmoid(gate) * up).astype(BF16)
        out = jnp.dot(mid, wd_ref[0], preferred_element_type=F32) * w_slot
        acc_ref[...] += lax.dot_general(sel, out.astype(BF16), tn_dims,
                                        preferred_element_type=F32)
        return carry

    lax.fori_loop(0, n_blocks, block, 0)

    @pl.when(e == N_EXPERTS - 1)
    def _():
        o_ref[...] = _layer_norm(alpha * h_ref[...] + acc_ref[...], g_ref[...], b_ref[...])


def _moe_call(h1, comb, w_gate, w_up, w_down, ln_g, ln_b, alpha):
    n = h1.shape[0]
    tm = MOE_TILE
    comb_t = comb[:, :N_EXPERTS].T
    wspec = pl.BlockSpec((1, D_MODEL, D_MODEL), lambda i, e: (e, 0, 0))
    row = pl.BlockSpec((1, D_MODEL), lambda i, e: (0, 0))
    return pl.pallas_call(
        functools.partial(_moe_kernel, alpha=alpha),
        out_shape=jax.ShapeDtypeStruct((n, D_MODEL), F32),
        grid=(n // tm, N_EXPERTS),
        in_specs=[pl.BlockSpec((tm, D_MODEL), lambda i, e: (i, 0),
                               pipeline_mode=pl.Buffered(1)),
                  pl.BlockSpec((N_EXPERTS, tm), lambda i, e: (0, i)),
                  wspec, wspec, wspec, row, row],
        out_specs=pl.BlockSpec((tm, D_MODEL), lambda i, e: (i, 0)),
        scratch_shapes=[pltpu.VMEM((tm, D_MODEL), BF16),
                        pltpu.VMEM((tm, D_MODEL), F32),
                        pltpu.VMEM((N_EXPERTS, tm), F32)],
        compiler_params=_params(("parallel", "arbitrary")),
        name="moe",
    )(h1, comb_t, w_gate, w_up, w_down, ln_g.reshape(1, -1), ln_b.reshape(1, -1))


def kernel(x, meta_tokens, ln_in_g, ln_in_b, w_in, conv_w, conv_b, wq_m, wk_m, wv_m, b_i, b_f, m_norm_g, lam_q1, lam_k1, lam_q2, lam_k2, a_norm_g, w_branch_m, w_branch_a, w_out, ln1_g, ln1_b, ln2_g, ln2_b, w_router, router_bias, w_gate_e, w_up_e, w_down_e):
    bsz, seq, dm = x.shape
    assert bsz == 1 and dm == D_MODEL
    depth = w_in.shape[0]
    alpha = float((2 * depth) ** 0.25)
    length = LEAD + seq
    n = -(-length // ROW_ALIGN) * ROW_ALIGN

    assert seq % LEAD == 0
    lead = jnp.concatenate([jnp.zeros((N_PAD, dm), F32), meta_tokens.astype(F32)], axis=0)
    h = _ln_call(x[0].astype(F32), lead, ln_in_g, ln_in_b, n)

    o_if = 2 * D_MODEL
    o_rest = o_if + 2 * M_HEADS
    for l in range(depth):
        w = w_in[l]
        q_scale = A_HEAD_DIM ** -0.5 * math.log2(math.e)
        blk = lambda b: w[:, o_rest + b * D_MODEL:o_rest + (b + 1) * D_MODEL]
        w_q, w_k, w_v, w_ga, w_gb = (blk(b) for b in range(5))
        w_main = jnp.concatenate([w[:, :o_if], w_k, w_ga, w_gb, w_q * q_scale, w_v],
                                 axis=1).astype(BF16)
        w_if = jnp.zeros((dm, LANES), F32).at[:, :2 * M_HEADS].set(w[:, o_if:o_rest]).astype(BF16)
        p, gif, q_t, v_t = _inproj_call(h, w_main, w_if)
        gif_t = gif[:, :2 * M_HEADS].T
        y_m = _mlstm_call(p, gif, gif_t, conv_w[l], conv_b[l], wq_m[l], wk_m[l], wv_m[l],
                          b_i[l], b_f[l], m_norm_g[l])
        lam_init = 0.8 - 0.6 * math.exp(-0.3 * l)
        y_a = _attn_call(p, q_t, v_t, lam_q1[l], lam_k1[l], lam_q2[l], lam_k2[l], a_norm_g[l],
                         lam_init)
        h1, comb = _merge_call(y_m, y_a, p, h, w_branch_m[l].astype(BF16),
                               w_branch_a[l].astype(BF16), w_out[l].astype(BF16),
                               ln1_g[l], ln1_b[l], w_router, router_bias, alpha)
        h = _moe_call(h1, comb, w_gate_e[l].astype(BF16), w_up_e[l].astype(BF16),
                      w_down_e[l].astype(BF16), ln2_g[l], ln2_b[l], alpha)
    return h[LEAD:length][None].astype(x.dtype)
```

```python
import functools
import math

import jax
import jax.numpy as jnp
from jax import lax
from jax.experimental import pallas as pl
from jax.experimental.pallas import tpu as pltpu

F32 = jnp.float32
BF16 = jnp.bfloat16

D_MODEL = 1024
N_META = 16
LEAD = 128
N_PAD = LEAD - N_META
M_HEADS = 4
M_HEAD_DIM = D_MODEL // M_HEADS
CONV_W = 4
A_HEADS = 8
A_HEAD_DIM = D_MODEL // (2 * A_HEADS)
A_VDIM = 2 * A_HEAD_DIM
N_EXPERTS = 16
N_GROUPS = 4
EXPERTS_PER_GROUP = N_EXPERTS // N_GROUPS
LN_EPS = 1e-5
NEG = -1e30

LANES = 128
BF16_SUBLANES = 16
V7X_VMEM_BYTES = 64 * 1024 * 1024
VMEM_LIMIT_BYTES = V7X_VMEM_BYTES * 7 // 8

LN_TILE = 512
LN_PIECES = LN_TILE // LEAD
PROJ_TILE_M = 1536
PROJ_TILE_N = 1024
M_CHUNK = 256
M_CHUNKS_PER_STEP = 2
ATTN_TQ = 1536
ATTN_TK = 768
ATTN_LANE_BLOCK = 256
MERGE_TILE = 768
MOE_TILE = 1536
MOE_SLOTS = 256
ROW_ALIGN = math.lcm(LN_TILE, PROJ_TILE_M, M_CHUNK * M_CHUNKS_PER_STEP, ATTN_TQ, ATTN_TK,
                     MERGE_TILE, MOE_TILE)

COL_XM, COL_OPRE, COL_K, COL_GA, COL_GB = range(5)
N_ROW_BLOCKS = 5


def _params(semantics):
    return pltpu.CompilerParams(dimension_semantics=semantics,
                                vmem_limit_bytes=VMEM_LIMIT_BYTES)


def _layer_norm(x, g, b):
    mu = jnp.mean(x, axis=-1, keepdims=True)
    xc = x - mu
    var = jnp.mean(xc * xc, axis=-1, keepdims=True)
    return xc * lax.rsqrt(var + LN_EPS) * g + b


def _sigmoid(x):
    return 1.0 / (1.0 + jnp.exp(-x))


def _ln_kernel(*refs, n_x_blocks):
    x_refs, (lead_ref, g_ref, b_ref, o_ref) = refs[:LN_PIECES], refs[LN_PIECES:]
    i = pl.program_id(0)
    for k, x_ref in enumerate(x_refs):
        blk = LN_PIECES * i - 1 + k
        rows = jnp.where(blk < 0, lead_ref[...], x_ref[...])
        rows = jnp.where(blk >= n_x_blocks, 0.0, rows)
        o_ref[k * LEAD:(k + 1) * LEAD, :] = _layer_norm(rows, g_ref[...], b_ref[...])


def _ln_call(x, lead, g, b, n):
    n_x_blocks = x.shape[0] // LEAD
    row = pl.BlockSpec((1, D_MODEL), lambda i: (0, 0))
    piece = lambda k: pl.BlockSpec(
        (LEAD, D_MODEL), lambda i: (jnp.clip(LN_PIECES * i - 1 + k, 0, n_x_blocks - 1), 0))
    return pl.pallas_call(
        functools.partial(_ln_kernel, n_x_blocks=n_x_blocks),
        out_shape=jax.ShapeDtypeStruct((n, D_MODEL), F32),
        grid=(n // LN_TILE,),
        in_specs=[piece(k) for k in range(LN_PIECES)]
        + [pl.BlockSpec((LEAD, D_MODEL), lambda i: (0, 0)), row, row],
        out_specs=pl.BlockSpec((LN_TILE, D_MODEL), lambda i: (i, 0)),
        compiler_params=_params(("parallel",)),
        name="ln_in",
    )(*([x] * LN_PIECES), lead, g.reshape(1, -1), b.reshape(1, -1))


def _inproj_kernel(h_ref, w_ref, wif_ref, p_ref, gif_ref, qt_ref, vt_ref, hb_ref):
    j = pl.program_id(1)
    tk = ATTN_TK

    @pl.when(j == 0)
    def _():
        hb_ref[...] = h_ref[...].astype(BF16)
        gif_ref[...] = jnp.dot(hb_ref[...], wif_ref[...], preferred_element_type=F32)

    def project():
        return jnp.dot(hb_ref[...], w_ref[...], preferred_element_type=F32)

    @pl.when(j < N_ROW_BLOCKS)
    def _():
        p_ref[...] = project().astype(BF16)

    def put_transposed(o_ref):
        res = project()
        for r in range(PROJ_TILE_M // tk):
            o_ref[r] = res[r * tk:(r + 1) * tk, :].T.astype(BF16)

    @pl.when(j == N_ROW_BLOCKS)
    def _():
        put_transposed(qt_ref)

    @pl.when(j == N_ROW_BLOCKS + 1)
    def _():
        put_transposed(vt_ref)


def _inproj_call(h, w_main, w_if):
    n = h.shape[0]
    tm, tk = PROJ_TILE_M, ATTN_TK
    t_shape = jax.ShapeDtypeStruct((n // tk, D_MODEL, tk), BF16)
    t_spec = pl.BlockSpec((tm // tk, D_MODEL, tk), lambda i, j: (i, 0, 0))
    return pl.pallas_call(
        _inproj_kernel,
        out_shape=(jax.ShapeDtypeStruct((n, N_ROW_BLOCKS * D_MODEL), BF16),
                   jax.ShapeDtypeStruct((n, LANES), F32), t_shape, t_shape),
        grid=(n // tm, w_main.shape[1] // PROJ_TILE_N),
        in_specs=[pl.BlockSpec((tm, D_MODEL), lambda i, j: (i, 0)),
                  pl.BlockSpec((D_MODEL, PROJ_TILE_N), lambda i, j: (0, j)),
                  pl.BlockSpec((D_MODEL, LANES), lambda i, j: (0, 0))],
        out_specs=(pl.BlockSpec((tm, PROJ_TILE_N),
                                lambda i, j: (i, jnp.minimum(j, N_ROW_BLOCKS - 1))),
                   pl.BlockSpec((tm, LANES), lambda i, j: (i, 0)), t_spec, t_spec),
        scratch_shapes=[pltpu.VMEM((tm, D_MODEL), BF16)],
        compiler_params=_params(("parallel", "arbitrary")),
        name="in_proj",
    )(h, w_main, w_if)


def _log_sigmoid(x):
    return jnp.minimum(x, 0.0) - jnp.log1p(jnp.exp(-jnp.abs(x)))


def _mlstm_kernel(xm_ref, halo_ref, opre_ref, gcol_ref, grow_ref, convw_ref, convb_ref,
                  wq_ref, wk_ref, wv_ref, bcol_ref, brow_ref, ng_ref,
                  y_ref, xs_ref, c_ref, n_ref, m_ref):
    c = pl.program_id(0)
    t = M_CHUNK
    rows = t * M_CHUNKS_PER_STEP
    halo = BF16_SUBLANES

    @pl.when(c == 0)
    def _():
        c_ref[...] = jnp.zeros_like(c_ref)
        n_ref[...] = jnp.zeros_like(n_ref)
        m_ref[...] = jnp.zeros_like(m_ref)

    pos_col = c * rows + lax.broadcasted_iota(jnp.int32, (rows, 1), 0)
    valid_col = pos_col >= N_PAD
    xm = jnp.where(valid_col, xm_ref[...].astype(F32), 0.0)
    halo_pos = c * rows - halo + lax.broadcasted_iota(jnp.int32, (halo, 1), 0)
    xs_ref[0:halo, :] = jnp.where(halo_pos >= N_PAD, halo_ref[...].astype(F32), 0.0)
    xs_ref[halo:halo + rows, :] = xm
    conv = xm * convw_ref[CONV_W - 1:CONV_W, :] + convb_ref[...]
    for k in range(CONV_W - 1):
        off = halo - (CONV_W - 1) + k
        conv = conv + xs_ref[off:off + rows, :] * convw_ref[k:k + 1, :]
    xc = conv * _sigmoid(conv)

    gc = gcol_ref[...] + bcol_ref[...]
    lane = lax.broadcasted_iota(jnp.int32, gc.shape, 1)
    lg_col = jnp.where(lane < M_HEADS,
                       jnp.where(valid_col, gc, NEG),
                       jnp.where(valid_col, _log_sigmoid(gc), 0.0))
    gr = grow_ref[...] + brow_ref[:, 0:1]
    valid_row = (c * rows + lax.broadcasted_iota(jnp.int32, (1, rows), 1)) >= N_PAD
    sub = lax.broadcasted_iota(jnp.int32, gr.shape, 0)
    lg_row = jnp.where(sub < M_HEADS,
                       jnp.where(valid_row, gr, NEG),
                       jnp.where(valid_row, _log_sigmoid(gr), 0.0))
    r_idx = lax.broadcasted_iota(jnp.int32, (t, t), 0)
    c_idx = lax.broadcasted_iota(jnp.int32, (t, t), 1)
    causal = c_idx <= r_idx
    tril = jnp.where(causal, 1.0, 0.0).astype(F32)
    triu = jnp.where(r_idx <= c_idx, 1.0, 0.0).astype(F32)

    heads = range(M_HEADS)
    chunks = range(M_CHUNKS_PER_STEP)
    pairs = [(s, h) for s in chunks for h in heads]
    rs = [slice(s * t, (s + 1) * t) for s in chunks]
    sls = [slice(h * M_HEAD_DIM, (h + 1) * M_HEAD_DIM) for h in heads]
    nt_dims = (((1,), (1,)), ((), ()))
    tn_dims = (((0,), (0,)), ((), ()))
    highest = lax.Precision.HIGHEST

    cum_col = [jnp.dot(tril, lg_col[rs[s], :], preferred_element_type=F32, precision=highest)
               for s in chunks]
    cum_row = [jnp.dot(lg_row[:, rs[s]], triu, preferred_element_type=F32, precision=highest)
               for s in chunks]

    b_c, li_c, b_r, li_r, g, w_state, decay = {}, {}, {}, {}, {}, {}, {}
    m_in = {(0, h): m_ref[h][:, 0:1] for h in heads}
    for s, h in pairs:
        b_c[s, h] = cum_col[s][:, M_HEADS + h:M_HEADS + h + 1]
        li_c[s, h] = lg_col[rs[s], h:h + 1]
        b_r[s, h] = cum_row[s][M_HEADS + h:M_HEADS + h + 1, :]
        li_r[s, h] = lg_row[h:h + 1, rs[s]]
        g[s, h] = b_r[s, h][:, t - 1:t]
        a = g[s, h] - b_c[s, h] + li_c[s, h]
        m_new = jnp.maximum(g[s, h] + m_in[s, h], jnp.max(a, axis=0, keepdims=True))
        w_state[s, h] = jnp.exp(a - m_new)
        decay[s, h] = jnp.exp(g[s, h] + m_in[s, h] - m_new)
        m_in[s + 1, h] = m_new

    q, k, qb, kb, vb = {}, {}, {}, {}, {}
    for s, h in pairs:
        xch = xc[rs[s], sls[h]].astype(BF16)
        q[s, h] = jnp.dot(xch, wq_ref[h], preferred_element_type=F32)
        k[s, h] = jnp.dot(xch, wk_ref[h], preferred_element_type=F32)
        v = jnp.dot(xm[rs[s], sls[h]].astype(BF16), wv_ref[h], preferred_element_type=F32)
        qb[s, h] = q[s, h].astype(BF16)
        kb[s, h] = k[s, h].astype(BF16)
        vb[s, h] = v.astype(BF16)

    m_t, scale_inter, num_intra, den_intra, kv, k_sum = {}, {}, {}, {}, {}, {}
    for s, h in pairs:
        d_mat = jnp.where(causal, b_c[s, h] - b_r[s, h] + li_r[s, h], NEG)
        inter = b_c[s, h] + m_in[s, h]
        m_t[s, h] = jnp.maximum(inter, jnp.max(d_mat, axis=-1, keepdims=True))
        sc = lax.dot_general(qb[s, h], kb[s, h], nt_dims, preferred_element_type=F32)
        w_intra = sc * jnp.exp(d_mat - m_t[s, h])
        scale_inter[s, h] = jnp.exp(inter - m_t[s, h])
        num_intra[s, h] = jnp.dot(w_intra.astype(BF16), vb[s, h], preferred_element_type=F32)
        den_intra[s, h] = jnp.sum(w_intra, axis=-1, keepdims=True)
        kw = k[s, h] * w_state[s, h]
        kv[s, h] = lax.dot_general(kw.astype(BF16), vb[s, h], tn_dims,
                                   preferred_element_type=F32)
        k_sum[s, h] = jnp.sum(kw, axis=0, keepdims=True)

    hh = {}
    c_state = {h: c_ref[h] for h in heads}
    n_state = {h: n_ref[h] for h in heads}
    for s, h in pairs:
        num = num_intra[s, h] + scale_inter[s, h] * jnp.dot(
            qb[s, h], c_state[h].astype(BF16), preferred_element_type=F32)
        den = den_intra[s, h] + scale_inter[s, h] * jnp.sum(
            q[s, h] * n_state[h], axis=-1, keepdims=True)
        hh[s, h] = num / jnp.maximum(jnp.abs(den), jnp.exp(-m_t[s, h]))
        c_state[h] = decay[s, h] * c_state[h] + kv[s, h]
        n_state[h] = decay[s, h] * n_state[h] + k_sum[s, h]
    for h in heads:
        c_ref[h] = c_state[h]
        n_ref[h] = n_state[h]
        m_ref[h] = jnp.broadcast_to(m_in[M_CHUNKS_PER_STEP, h], (1, LANES))

    for s, h in pairs:
        gated = _sigmoid(opre_ref[rs[s], sls[h]].astype(F32)) * hh[s, h]
        mu = jnp.mean(gated, axis=-1, keepdims=True)
        hc = gated - mu
        var = jnp.mean(hc * hc, axis=-1, keepdims=True)
        y_ref[rs[s], sls[h]] = (hc * lax.rsqrt(var + LN_EPS) * ng_ref[:, sls[h]]).astype(BF16)


def _mlstm_call(p, gif, gif_t, conv_w, conv_b, wq, wk, wv, b_i, b_f, norm_g):
    n = p.shape[0]
    t = M_CHUNK * M_CHUNKS_PER_STEP
    halo = BF16_SUBLANES
    bias = jnp.concatenate([b_i, b_f]).astype(F32)
    bias_col = jnp.zeros((1, LANES), F32).at[0, :2 * M_HEADS].set(bias)
    bias_row = jnp.broadcast_to(bias[:, None], (2 * M_HEADS, LANES))
    const = lambda *shape: pl.BlockSpec(shape, lambda c: (0,) * len(shape))
    return pl.pallas_call(
        _mlstm_kernel,
        out_shape=jax.ShapeDtypeStruct((n, D_MODEL), BF16),
        grid=(n // t,),
        in_specs=[
            pl.BlockSpec((t, D_MODEL), lambda c: (c, COL_XM)),
            pl.BlockSpec((halo, D_MODEL),
                         lambda c: (jnp.maximum(c * (t // halo) - 1, 0), COL_XM)),
            pl.BlockSpec((t, D_MODEL), lambda c: (c, COL_OPRE)),
            pl.BlockSpec((t, LANES), lambda c: (c, 0)),
            pl.BlockSpec((2 * M_HEADS, t), lambda c: (0, c)),
            const(CONV_W, D_MODEL),
            const(1, D_MODEL),
            const(M_HEADS, M_HEAD_DIM, M_HEAD_DIM),
            const(M_HEADS, M_HEAD_DIM, M_HEAD_DIM),
            const(M_HEADS, M_HEAD_DIM, M_HEAD_DIM),
            const(1, LANES),
            const(2 * M_HEADS, LANES),
            const(1, D_MODEL),
        ],
        out_specs=pl.BlockSpec((t, D_MODEL), lambda c: (c, 0)),
        scratch_shapes=[
            pltpu.VMEM((t + halo, D_MODEL), F32),
            pltpu.VMEM((M_HEADS, M_HEAD_DIM, M_HEAD_DIM), F32),
            pltpu.VMEM((M_HEADS, 1, M_HEAD_DIM), F32),
            pltpu.VMEM((M_HEADS, 1, LANES), F32),
        ],
        compiler_params=_params(("arbitrary",)),
        name="mlstm",
    )(p, p, p, gif, gif_t, conv_w, conv_b.reshape(1, -1),
      wq.astype(BF16), (wk * M_HEAD_DIM ** -0.5).astype(BF16), wv.astype(BF16),
      bias_col, bias_row,
      norm_g.reshape(1, -1))


def _attn_kernel(qt_ref, k_ref, vt_ref, lq1_ref, lk1_ref, lq2_ref, lk2_ref, g_ref,
                 o_ref, qz_ref, s0_ref, s1_ref, mt0_ref, mt1_ref, m_ref, acc_ref,
                 *, lam_init):
    i = pl.program_id(1)
    tq, tk = ATTN_TQ, ATTN_TK
    s_refs = (s0_ref, s1_ref)
    mt_refs = (mt0_ref, mt1_ref)

    for r in range(tq // tk):
        qt = qt_ref[r]
        sub = lax.broadcasted_iota(jnp.int32, qt.shape, 0)
        zero = jnp.zeros_like(qt)
        qz_ref[:, r * tk:(r + 1) * tk] = jnp.where(sub < A_HEAD_DIM, qt, zero)
        qz_ref[:, tq + r * tk:tq + (r + 1) * tk] = jnp.where(sub >= A_HEAD_DIM, qt, zero)
    m_ref[...] = jnp.full_like(m_ref, NEG)
    acc_ref[...] = jnp.zeros_like(acc_ref)

    n_blocks = 2 * tq // ATTN_LANE_BLOCK
    lane_blocks = [slice(b * ATTN_LANE_BLOCK, (b + 1) * ATTN_LANE_BLOCK) for b in range(n_blocks)]

    n_diag = tq // tk
    LEAD_TILE = "lead"
    NO_TILE = "no tile"

    def block_state(kind, b):
        if kind is None:
            return "open"
        if kind == LEAD_TILE:
            return "lead"
        q_lo = (b * ATTN_LANE_BLOCK) % tq
        if q_lo + ATTN_LANE_BLOCK <= kind * tk:
            return "skip"
        if q_lo >= (kind + 1) * tk - 1:
            return "open"
        return "causal"

    def scores_block(j, buf, kind, b):
        state = block_state(kind, b)
        if state == "skip":
            return
        lanes = lane_blocks[b]
        start = pl.multiple_of(j * tk, tk)
        kt = k_ref[pl.ds(start, tk), :]
        s = jnp.dot(kt, qz_ref[:, lanes], preferred_element_type=F32)
        if state == "lead":
            kpos = j * tk + lax.broadcasted_iota(jnp.int32, s.shape, 0)
            col = b * ATTN_LANE_BLOCK + lax.broadcasted_iota(jnp.int32, s.shape, 1)
            qpos = i * tq + jnp.where(col >= tq, col - tq, col)
            s = jnp.where(kpos <= qpos, jnp.where(kpos >= N_PAD, s, NEG), NEG)
        elif state == "causal":
            k_rel = kind * tk + lax.broadcasted_iota(jnp.int32, s.shape, 0)
            q_rel = ((b * ATTN_LANE_BLOCK) % tq
                     + lax.broadcasted_iota(jnp.int32, s.shape, 1))
            s = jnp.where(k_rel <= q_rel, s, NEG)
        s_refs[buf][:, lanes] = s
        mt_refs[buf][:, lanes] = jnp.max(s, axis=0, keepdims=True)

    ones_rows = jnp.ones((BF16_SUBLANES, tk), BF16)

    def accumulate_block(j, buf, kind, b):
        if block_state(kind, b) == "skip":
            return
        lanes = lane_blocks[b]
        m_prev = m_ref[:, lanes]
        m_new = jnp.maximum(m_prev, mt_refs[buf][:, lanes])
        alpha = jnp.exp2(m_prev - m_new)
        p = jnp.exp2((s_refs[buf][:, lanes] - m_new).astype(BF16))
        m_ref[:, lanes] = m_new
        vt = jnp.concatenate([vt_ref[j], ones_rows], axis=0)
        pv = jnp.dot(vt, p, preferred_element_type=F32)
        acc_ref[:, lanes] = alpha * acc_ref[:, lanes] + pv

    def stage(j, kind, kind_next):
        for buf in (0, 1):
            @pl.when(j % 2 == buf)
            def _():
                for b in range(n_blocks):
                    accumulate_block(j, buf, kind, b)
                    if kind_next != NO_TILE:
                        scores_block(j + 1, 1 - buf, kind_next, b)

    n_k = (i + 1) * n_diag
    first_diag = n_k - n_diag
    for b in range(n_blocks):
        scores_block(0, 0, LEAD_TILE, b)

    def body(j, carry):
        stage(j, None, None)
        return carry

    lax.fori_loop(0, first_diag - 1, body, 0)

    @pl.when(first_diag > 0)
    def _():
        stage(first_diag - 1, None, 0)

    for d in range(1, n_diag):
        stage(first_diag - 1 + d, d - 1, d)
    stage(n_k - 1, n_diag - 1, NO_TILE)

    lam = (jnp.exp(jnp.sum(lq1_ref[...] * lk1_ref[...], axis=-1, keepdims=True))
           - jnp.exp(jnp.sum(lq2_ref[...] * lk2_ref[...], axis=-1, keepdims=True))
           + lam_init)
    out = acc_ref[0:A_VDIM, :] / acc_ref[A_VDIM:A_VDIM + 1, :]
    o = out[:, 0:tq] - lam * out[:, tq:2 * tq]
    o = o * lax.rsqrt(jnp.mean(o * o, axis=0, keepdims=True) + LN_EPS)
    o = o * g_ref[...] * (1.0 - lam_init)
    o_ref[...] = o.T.astype(BF16)


def _attn_call(p, q_t, v_t, lq1, lk1, lq2, lk2, norm_g, lam_init):
    n = p.shape[0]
    tq, tk = ATTN_TQ, ATTN_TK
    hpb = D_MODEL // A_VDIM
    vec = lambda d: pl.BlockSpec((1, d), lambda h, i: (0, 0))
    return pl.pallas_call(
        functools.partial(_attn_kernel, lam_init=lam_init),
        out_shape=jax.ShapeDtypeStruct((n, D_MODEL), BF16),
        grid=(A_HEADS, n // tq),
        in_specs=[
            pl.BlockSpec((tq // tk, A_VDIM, tk), lambda h, i: (i, h, 0)),
            pl.BlockSpec((n, A_VDIM), lambda h, i: (0, COL_K * hpb + h)),
            pl.BlockSpec((n // tk, A_VDIM, tk), lambda h, i: (0, h, 0)),
            vec(A_HEAD_DIM), vec(A_HEAD_DIM), vec(A_HEAD_DIM), vec(A_HEAD_DIM),
            pl.BlockSpec((A_VDIM, 1), lambda h, i: (0, 0)),
        ],
        out_specs=pl.BlockSpec((tq, A_VDIM), lambda h, i: (i, h)),
        scratch_shapes=[
            pltpu.VMEM((A_VDIM, 2 * tq), BF16),
            pltpu.VMEM((tk, 2 * tq), F32),
            pltpu.VMEM((tk, 2 * tq), F32),
            pltpu.VMEM((1, 2 * tq), F32),
            pltpu.VMEM((1, 2 * tq), F32),
            pltpu.VMEM((1, 2 * tq), F32),
            pltpu.VMEM((A_VDIM + BF16_SUBLANES, 2 * tq), F32),
        ],
        compiler_params=_params(("parallel", "arbitrary")),
        name="diff_attn",
    )(q_t, p, v_t, lq1.reshape(1, -1), lk1.reshape(1, -1), lq2.reshape(1, -1),
      lk2.reshape(1, -1), norm_g.reshape(-1, 1))


def _merge_kernel(ym_ref, ya_ref, ga_ref, gb_ref, h_ref, wm_ref, wa_ref, wo_ref,
                  g_ref, b_ref, wr_ref, rb_ref, h1_ref, comb_ref, *, alpha):
    bm = jnp.dot(ym_ref[...], wm_ref[...], preferred_element_type=F32)
    ba = jnp.dot(ya_ref[...], wa_ref[...], preferred_element_type=F32)
    merged = (_sigmoid(ga_ref[...].astype(F32)) * bm
              + _sigmoid(gb_ref[...].astype(F32)) * ba)
    proj = jnp.dot(merged.astype(BF16), wo_ref[...], preferred_element_type=F32)
    h1 = _layer_norm(alpha * h_ref[...] + proj, g_ref[...], b_ref[...])
    h1_ref[...] = h1
    comb_ref[...] = _route(h1, wr_ref, rb_ref)


def _route(h1, wr_ref, rb_ref):
    h_hi = h1.astype(BF16)
    h_lo = (h1 - h_hi.astype(F32)).astype(BF16)
    r_hi = jnp.dot(h_hi, wr_ref[...], preferred_element_type=F32)
    r_lo = jnp.dot(h_lo, wr_ref[:, 0:LANES], preferred_element_type=F32)
    logits = (r_hi[:, 0:LANES] + r_lo) + r_hi[:, LANES:2 * LANES]
    lane = lax.broadcasted_iota(jnp.int32, logits.shape, 1)
    is_expert = lane < N_EXPERTS
    logits = jnp.where(is_expert, logits, NEG)
    ex = jnp.exp(logits - jnp.max(logits, axis=-1, keepdims=True))
    probs = ex / jnp.sum(ex, axis=-1, keepdims=True)
    sel = jnp.where(is_expert, probs + rb_ref[...], NEG)
    group = lane // EXPERTS_PER_GROUP
    best = jnp.max(jnp.where(group == 0, sel, NEG), axis=-1, keepdims=True)
    top_group = jnp.zeros_like(best, dtype=jnp.int32)
    for gidx in range(1, N_GROUPS):
        score = jnp.max(jnp.where(group == gidx, sel, NEG), axis=-1, keepdims=True)
        top_group = jnp.where(score > best, gidx, top_group)
        best = jnp.maximum(best, score)
    cand = jnp.where(group == top_group, sel, NEG)
    v1 = jnp.max(cand, axis=-1, keepdims=True)
    i1 = jnp.min(jnp.where(cand == v1, lane, LANES), axis=-1, keepdims=True)
    cand2 = jnp.where(lane == i1, NEG, cand)
    v2 = jnp.max(cand2, axis=-1, keepdims=True)
    i2 = jnp.min(jnp.where(cand2 == v2, lane, LANES), axis=-1, keepdims=True)
    p1 = jnp.sum(jnp.where(lane == i1, probs, 0.0), axis=-1, keepdims=True)
    p2 = jnp.sum(jnp.where(lane == i2, probs, 0.0), axis=-1, keepdims=True)
    tot = p1 + p2
    return jnp.where(lane == i1, p1 / tot, 0.0) + jnp.where(lane == i2, p2 / tot, 0.0)


def _merge_call(y_m, y_a, p, h, w_m, w_a, w_o, ln_g, ln_b, w_router, router_bias, alpha):
    n = h.shape[0]
    tm = MERGE_TILE
    full = pl.BlockSpec((D_MODEL, D_MODEL), lambda i: (0, 0))
    row = pl.BlockSpec((1, D_MODEL), lambda i: (0, 0))
    tile = lambda col: pl.BlockSpec((tm, D_MODEL), lambda i: (i, col))
    wr32 = jnp.zeros((D_MODEL, LANES), F32).at[:, :N_EXPERTS].set(w_router)
    wr_hi = wr32.astype(BF16)
    wr = jnp.concatenate([wr_hi, (wr32 - wr_hi.astype(F32)).astype(BF16)], axis=1)
    rb = jnp.zeros((1, LANES), F32).at[0, :N_EXPERTS].set(router_bias)
    return pl.pallas_call(
        functools.partial(_merge_kernel, alpha=alpha),
        out_shape=(jax.ShapeDtypeStruct((n, D_MODEL), F32),
                   jax.ShapeDtypeStruct((n, LANES), F32)),
        grid=(n // tm,),
        in_specs=[tile(0), tile(0), tile(COL_GA), tile(COL_GB), tile(0),
                  full, full, full, row, row,
                  pl.BlockSpec((D_MODEL, 2 * LANES), lambda i: (0, 0)),
                  pl.BlockSpec((1, LANES), lambda i: (0, 0))],
        out_specs=(tile(0), pl.BlockSpec((tm, LANES), lambda i: (i, 0))),
        compiler_params=_params(("parallel",)),
        name="merge_router",
    )(y_m, y_a, p, p, h, w_m, w_a, w_o, ln_g.reshape(1, -1), ln_b.reshape(1, -1), wr, rb)


def _moe_kernel(h_ref, combt_ref, wg_ref, wu_ref, wd_ref, g_ref, b_ref, o_ref,
                xb_ref, acc_ref, rankr_ref, *, alpha):
    e = pl.program_id(1)
    tm = MOE_TILE
    ns = MOE_SLOTS

    @pl.when(e == 0)
    def _():
        xb_ref[...] = h_ref[...].astype(BF16)
        acc_ref[...] = jnp.zeros_like(acc_ref)
        r_idx = lax.broadcasted_iota(jnp.int32, (tm, tm), 0)
        c_idx = lax.broadcasted_iota(jnp.int32, (tm, tm), 1)
        after = jnp.where(r_idx < c_idx, 1.0, 0.0).astype(BF16)
        rankr_ref[...] = jnp.dot(jnp.where(combt_ref[...] > 0.0, 1.0, 0.0).astype(BF16), after,
                                 preferred_element_type=F32)

    w_row = combt_ref[pl.ds(e, 1), :]
    routed_r = w_row > 0.0
    rank_r = rankr_ref[pl.ds(e, 1), :]
    count = jnp.sum(jnp.where(routed_r, 1.0, 0.0)).astype(jnp.int32)
    n_blocks = (count + ns - 1) // ns
    tn_dims = (((0,), (0,)), ((), ()))

    def block(b, carry):
        base = (b * ns).astype(F32)
        slot_r = lax.broadcasted_iota(jnp.int32, (ns, tm), 0).astype(F32) + base
        hit = routed_r & (rank_r == slot_r)
        sel = jnp.where(hit, 1.0, 0.0).astype(BF16)
        w_slot = jnp.sum(jnp.where(hit, w_row, 0.0), axis=-1, keepdims=True)
        xs = jnp.dot(sel, xb_ref[...], preferred_element_type=F32).astype(BF16)
        gate = jnp.dot(xs, wg_ref[0], preferred_element_type=F32)
        up = jnp.dot(xs, wu_ref[0], preferred_element_type=F32)
        mid = (gate * _sigmoid(gate) * up).astype(BF16)
        out = jnp.dot(mid, wd_ref[0], preferred_element_type=F32) * w_slot
        acc_ref[...] += lax.dot_general(sel, out.astype(BF16), tn_dims,
                                        preferred_element_type=F32)
        return carry

    lax.fori_loop(0, n_blocks, block, 0)

    @pl.when(e == N_EXPERTS - 1)
    def _():
        o_ref[...] = _layer_norm(alpha * h_ref[...] + acc_ref[...], g_ref[...], b_ref[...])


def _moe_call(h1, comb, w_gate, w_up, w_down, ln_g, ln_b, alpha):
    n = h1.shape[0]
    tm = MOE_TILE
    comb_t = comb[:, :N_EXPERTS].T
    wspec = pl.BlockSpec((1, D_MODEL, D_MODEL), lambda i, e: (e, 0, 0))
    row = pl.BlockSpec((1, D_MODEL), lambda i, e: (0, 0))
    return pl.pallas_call(
        functools.partial(_moe_kernel, alpha=alpha),
        out_shape=jax.ShapeDtypeStruct((n, D_MODEL), F32),
        grid=(n // tm, N_EXPERTS),
        in_specs=[pl.BlockSpec((tm, D_MODEL), lambda i, e: (i, 0),
                               pipeline_mode=pl.Buffered(1)),
                  pl.BlockSpec((N_EXPERTS, tm), lambda i, e: (0, i)),
                  wspec, wspec, wspec, row, row],
        out_specs=pl.BlockSpec((tm, D_MODEL), lambda i, e: (i, 0)),
        scratch_shapes=[pltpu.VMEM((tm, D_MODEL), BF16),
                        pltpu.VMEM((tm, D_MODEL), F32),
                        pltpu.VMEM((N_EXPERTS, tm), F32)],
        compiler_params=_params(("parallel", "arbitrary")),
        name="moe",
    )(h1, comb_t, w_gate, w_up, w_down, ln_g.reshape(1, -1), ln_b.reshape(1, -1))


def kernel(x, meta_tokens, ln_in_g, ln_in_b, w_in, conv_w, conv_b, wq_m, wk_m, wv_m, b_i, b_f, m_norm_g, lam_q1, lam_k1, lam_q2, lam_k2, a_norm_g, w_branch_m, w_branch_a, w_out, ln1_g, ln1_b, ln2_g, ln2_b, w_router, router_bias, w_gate_e, w_up_e, w_down_e):
    bsz, seq, dm = x.shape
    assert bsz == 1 and dm == D_MODEL
    depth = w_in.shape[0]
    alpha = float((2 * depth) ** 0.25)
    length = LEAD + seq
    n = -(-length // ROW_ALIGN) * ROW_ALIGN

    assert seq % LEAD == 0
    lead = jnp.concatenate([jnp.zeros((N_PAD, dm), F32), meta_tokens.astype(F32)], axis=0)
    h = _ln_call(x[0].astype(F32), lead, ln_in_g, ln_in_b, n)

    o_if = 2 * D_MODEL
    o_rest = o_if + 2 * M_HEADS
    for l in range(depth):
        w = w_in[l]
        q_scale = A_HEAD_DIM ** -0.5 * math.log2(math.e)
        blk = lambda b: w[:, o_rest + b * D_MODEL:o_rest + (b + 1) * D_MODEL]
        w_q, w_k, w_v, w_ga, w_gb = (blk(b) for b in range(5))
        w_main = jnp.concatenate([w[:, :o_if], w_k, w_ga, w_gb, w_q * q_scale, w_v],
                                 axis=1).astype(BF16)
        w_if = jnp.zeros((dm, LANES), F32).at[:, :2 * M_HEADS].set(w[:, o_if:o_rest]).astype(BF16)
        p, gif, q_t, v_t = _inproj_call(h, w_main, w_if)
        gif_t = gif[:, :2 * M_HEADS].T
        y_m = _mlstm_call(p, gif, gif_t, conv_w[l], conv_b[l], wq_m[l], wk_m[l], wv_m[l],
                          b_i[l], b_f[l], m_norm_g[l])
        lam_init = 0.8 - 0.6 * math.exp(-0.3 * l)
        y_a = _attn_call(p, q_t, v_t, lam_q1[l], lam_k1[l], lam_q2[l], lam_k2[l], a_norm_g[l],
                         lam_init)
        h1, comb = _merge_call(y_m, y_a, p, h, w_branch_m[l].astype(BF16),
                               w_branch_a[l].astype(BF16), w_out[l].astype(BF16),
                               ln1_g[l], ln1_b[l], w_router, router_bias, alpha)
        h = _moe_call(h1, comb, w_gate_e[l].astype(BF16), w_up_e[l].astype(BF16),
                      w_down_e[l].astype(BF16), ln2_g[l], ln2_b[l], alpha)
    return h[LEAD:length][None].astype(x.dtype)
```

```python
import functools
import math

import jax
import jax.numpy as jnp
from jax import lax
from jax.experimental import pallas as pl
from jax.experimental.pallas import tpu as pltpu

F32 = jnp.float32
BF16 = jnp.bfloat16

D_MODEL = 1024
N_META = 16
LEAD = 128
N_PAD = LEAD - N_META
M_HEADS = 4
M_HEAD_DIM = D_MODEL // M_HEADS
CONV_W = 4
A_HEADS = 8
A_HEAD_DIM = D_MODEL // (2 * A_HEADS)
A_VDIM = 2 * A_HEAD_DIM
N_EXPERTS = 16
N_GROUPS = 4
EXPERTS_PER_GROUP = N_EXPERTS // N_GROUPS
LN_EPS = 1e-5
NEG = -1e30

LANES = 128
BF16_SUBLANES = 16
V7X_VMEM_BYTES = 64 * 1024 * 1024
VMEM_LIMIT_BYTES = V7X_VMEM_BYTES * 7 // 8

LN_TILE = 512
LN_PIECES = LN_TILE // LEAD
PROJ_TILE_M = 1536
PROJ_TILE_N = 1024
M_CHUNK = 256
M_CHUNKS_PER_STEP = 2
ATTN_TQ = 1536
ATTN_TK = 768
ATTN_LANE_BLOCK = 256
MERGE_TILE = 768
MOE_TILE = 1536
MOE_SLOTS = 256
ROW_ALIGN = math.lcm(LN_TILE, PROJ_TILE_M, M_CHUNK * M_CHUNKS_PER_STEP, ATTN_TQ, ATTN_TK,
                     MERGE_TILE, MOE_TILE)

COL_XM, COL_OPRE, COL_K, COL_GA, COL_GB = range(5)
N_ROW_BLOCKS = 5


def _params(semantics):
    return pltpu.CompilerParams(dimension_semantics=semantics,
                                vmem_limit_bytes=VMEM_LIMIT_BYTES)


def _layer_norm(x, g, b):
    mu = jnp.mean(x, axis=-1, keepdims=True)
    xc = x - mu
    var = jnp.mean(xc * xc, axis=-1, keepdims=True)
    return xc * lax.rsqrt(var + LN_EPS) * g + b


def _sigmoid(x):
    return 1.0 / (1.0 + jnp.exp(-x))


def _ln_kernel(*refs, n_x_blocks):
    x_refs, (lead_ref, g_ref, b_ref, o_ref) = refs[:LN_PIECES], refs[LN_PIECES:]
    i = pl.program_id(0)
    for k, x_ref in enumerate(x_refs):
        blk = LN_PIECES * i - 1 + k
        rows = jnp.where(blk < 0, lead_ref[...], x_ref[...])
        rows = jnp.where(blk >= n_x_blocks, 0.0, rows)
        o_ref[k * LEAD:(k + 1) * LEAD, :] = _layer_norm(rows, g_ref[...], b_ref[...])


def _ln_call(x, lead, g, b, n):
    n_x_blocks = x.shape[0] // LEAD
    row = pl.BlockSpec((1, D_MODEL), lambda i: (0, 0))
    piece = lambda k: pl.BlockSpec(
        (LEAD, D_MODEL), lambda i: (jnp.clip(LN_PIECES * i - 1 + k, 0, n_x_blocks - 1), 0))
    return pl.pallas_call(
        functools.partial(_ln_kernel, n_x_blocks=n_x_blocks),
        out_shape=jax.ShapeDtypeStruct((n, D_MODEL), F32),
        grid=(n // LN_TILE,),
        in_specs=[piece(k) for k in range(LN_PIECES)]
        + [pl.BlockSpec((LEAD, D_MODEL), lambda i: (0, 0)), row, row],
        out_specs=pl.BlockSpec((LN_TILE, D_MODEL), lambda i: (i, 0)),
        compiler_params=_params(("parallel",)),
        name="ln_in",
    )(*([x] * LN_PIECES), lead, g.reshape(1, -1), b.reshape(1, -1))


def _inproj_kernel(h_ref, w_ref, wif_ref, p_ref, gif_ref, qt_ref, vt_ref, hb_ref):
    j = pl.program_id(1)
    tk = ATTN_TK

    @pl.when(j == 0)
    def _():
        hb_ref[...] = h_ref[...].astype(BF16)
        gif_ref[...] = jnp.dot(hb_ref[...], wif_ref[...], preferred_element_type=F32)

    def project():
        return jnp.dot(hb_ref[...], w_ref[...], preferred_element_type=F32)

    @pl.when(j < N_ROW_BLOCKS)
    def _():
        p_ref[...] = project().astype(BF16)

    def put_transposed(o_ref):
        res = project()
        for r in range(PROJ_TILE_M // tk):
            o_ref[r] = res[r * tk:(r + 1) * tk, :].T.astype(BF16)

    @pl.when(j == N_ROW_BLOCKS)
    def _():
        put_transposed(qt_ref)

    @pl.when(j == N_ROW_BLOCKS + 1)
    def _():
        put_transposed(vt_ref)


def _inproj_call(h, w_main, w_if):
    n = h.shape[0]
    tm, tk = PROJ_TILE_M, ATTN_TK
    t_shape = jax.ShapeDtypeStruct((n // tk, D_MODEL, tk), BF16)
    t_spec = pl.BlockSpec((tm // tk, D_MODEL, tk), lambda i, j: (i, 0, 0))
    return pl.pallas_call(
        _inproj_kernel,
        out_shape=(jax.ShapeDtypeStruct((n, N_ROW_BLOCKS * D_MODEL), BF16),
                   jax.ShapeDtypeStruct((n, LANES), F32), t_shape, t_shape),
        grid=(n // tm, w_main.shape[1] // PROJ_TILE_N),
        in_specs=[pl.BlockSpec((tm, D_MODEL), lambda i, j: (i, 0)),
                  pl.BlockSpec((D_MODEL, PROJ_TILE_N), lambda i, j: (0, j)),
                  pl.BlockSpec((D_MODEL, LANES), lambda i, j: (0, 0))],
        out_specs=(pl.BlockSpec((tm, PROJ_TILE_N),
                                lambda i, j: (i, jnp.minimum(j, N_ROW_BLOCKS - 1))),
                   pl.BlockSpec((tm, LANES), lambda i, j: (i, 0)), t_spec, t_spec),
        scratch_shapes=[pltpu.VMEM((tm, D_MODEL), BF16)],
        compiler_params=_params(("parallel", "arbitrary")),
        name="in_proj",
    )(h, w_main, w_if)


def _log_sigmoid(x):
    return jnp.minimum(x, 0.0) - jnp.log1p(jnp.exp(-jnp.abs(x)))


def _mlstm_kernel(xm_ref, halo_ref, opre_ref, gcol_ref, grow_ref, convw_ref, convb_ref,
                  wq_ref, wk_ref, wv_ref, bcol_ref, brow_ref, ng_ref,
                  y_ref, xs_ref, c_ref, n_ref, m_ref):
    c = pl.program_id(0)
    t = M_CHUNK
    rows = t * M_CHUNKS_PER_STEP
    halo = BF16_SUBLANES

    @pl.when(c == 0)
    def _():
        c_ref[...] = jnp.zeros_like(c_ref)
        n_ref[...] = jnp.zeros_like(n_ref)
        m_ref[...] = jnp.zeros_like(m_ref)

    pos_col = c * rows + lax.broadcasted_iota(jnp.int32, (rows, 1), 0)
    valid_col = pos_col >= N_PAD
    xm = jnp.where(valid_col, xm_ref[...].astype(F32), 0.0)
    halo_pos = c * rows - halo + lax.broadcasted_iota(jnp.int32, (halo, 1), 0)
    xs_ref[0:halo, :] = jnp.where(halo_pos >= N_PAD, halo_ref[...].astype(F32), 0.0)
    xs_ref[halo:halo + rows, :] = xm
    conv = xm * convw_ref[CONV_W - 1:CONV_W, :] + convb_ref[...]
    for k in range(CONV_W - 1):
        off = halo - (CONV_W - 1) + k
        conv = conv + xs_ref[off:off + rows, :] * convw_ref[k:k + 1, :]
    xc = conv * _sigmoid(conv)

    gc = gcol_ref[...] + bcol_ref[...]
    lane = lax.broadcasted_iota(jnp.int32, gc.shape, 1)
    lg_col = jnp.where(lane < M_HEADS,
                       jnp.where(valid_col, gc, NEG),
                       jnp.where(valid_col, _log_sigmoid(gc), 0.0))
    gr = grow_ref[...] + brow_ref[:, 0:1]
    valid_row = (c * rows + lax.broadcasted_iota(jnp.int32, (1, rows), 1)) >= N_PAD
    sub = lax.broadcasted_iota(jnp.int32, gr.shape, 0)
    lg_row = jnp.where(sub < M_HEADS,
                       jnp.where(valid_row, gr, NEG),
                       jnp.where(valid_row, _log_sigmoid(gr), 0.0))
    r_idx = lax.broadcasted_iota(jnp.int32, (t, t), 0)
    c_idx = lax.broadcasted_iota(jnp.int32, (t, t), 1)
    causal = c_idx <= r_idx
    tril = jnp.where(causal, 1.0, 0.0).astype(F32)
    triu = jnp.where(r_idx <= c_idx, 1.0, 0.0).astype(F32)

    heads = range(M_HEADS)
    chunks = range(M_CHUNKS_PER_STEP)
    pairs = [(s, h) for s in chunks for h in heads]
    rs = [slice(s * t, (s + 1) * t) for s in chunks]
    sls = [slice(h * M_HEAD_DIM, (h + 1) * M_HEAD_DIM) for h in heads]
    nt_dims = (((1,), (1,)), ((), ()))
    tn_dims = (((0,), (0,)), ((), ()))
    highest = lax.Precision.HIGHEST

    cum_col = [jnp.dot(tril, lg_col[rs[s], :], preferred_element_type=F32, precision=highest)
               for s in chunks]
    cum_row = [jnp.dot(lg_row[:, rs[s]], triu, preferred_element_type=F32, precision=highest)
               for s in chunks]

    b_c, li_c, b_r, li_r, g, w_state, decay = {}, {}, {}, {}, {}, {}, {}
    m_in = {(0, h): m_ref[h][:, 0:1] for h in heads}
    for s, h in pairs:
        b_c[s, h] = cum_col[s][:, M_HEADS + h:M_HEADS + h + 1]
        li_c[s, h] = lg_col[rs[s], h:h + 1]
        b_r[s, h] = cum_row[s][M_HEADS + h:M_HEADS + h + 1, :]
        li_r[s, h] = lg_row[h:h + 1, rs[s]]
        g[s, h] = b_r[s, h][:, t - 1:t]
        a = g[s, h] - b_c[s, h] + li_c[s, h]
        m_new = jnp.maximum(g[s, h] + m_in[s, h], jnp.max(a, axis=0, keepdims=True))
        w_state[s, h] = jnp.exp(a - m_new)
        decay[s, h] = jnp.exp(g[s, h] + m_in[s, h] - m_new)
        m_in[s + 1, h] = m_new

    q, k, qb, kb, vb = {}, {}, {}, {}, {}
    for s, h in pairs:
        xch = xc[rs[s], sls[h]].astype(BF16)
        q[s, h] = jnp.dot(xch, wq_ref[h], preferred_element_type=F32)
        k[s, h] = jnp.dot(xch, wk_ref[h], preferred_element_type=F32)
        v = jnp.dot(xm[rs[s], sls[h]].astype(BF16), wv_ref[h], preferred_element_type=F32)
        qb[s, h] = q[s, h].astype(BF16)
        kb[s, h] = k[s, h].astype(BF16)
        vb[s, h] = v.astype(BF16)

    m_t, scale_inter, num_intra, den_intra, kv, k_sum = {}, {}, {}, {}, {}, {}
    for s, h in pairs:
        d_mat = jnp.where(causal, b_c[s, h] - b_r[s, h] + li_r[s, h], NEG)
        inter = b_c[s, h] + m_in[s, h]
        m_t[s, h] = jnp.maximum(inter, jnp.max(d_mat, axis=-1, keepdims=True))
        sc = lax.dot_general(qb[s, h], kb[s, h], nt_dims, preferred_element_type=F32)
        w_intra = sc * jnp.exp(d_mat - m_t[s, h])
        scale_inter[s, h] = jnp.exp(inter - m_t[s, h])
        num_intra[s, h] = jnp.dot(w_intra.astype(BF16), vb[s, h], preferred_element_type=F32)
        den_intra[s, h] = jnp.sum(w_intra, axis=-1, keepdims=True)
        kw = k[s, h] * w_state[s, h]
        kv[s, h] = lax.dot_general(kw.astype(BF16), vb[s, h], tn_dims,
                                   preferred_element_type=F32)
        k_sum[s, h] = jnp.sum(kw, axis=0, keepdims=True)

    hh = {}
    c_state = {h: c_ref[h] for h in heads}
    n_state = {h: n_ref[h] for h in heads}
    for s, h in pairs:
        num = num_intra[s, h] + scale_inter[s, h] * jnp.dot(
            qb[s, h], c_state[h].astype(BF16), preferred_element_type=F32)
        den = den_intra[s, h] + scale_inter[s, h] * jnp.sum(
            q[s, h] * n_state[h], axis=-1, keepdims=True)
        hh[s, h] = num / jnp.maximum(jnp.abs(den), jnp.exp(-m_t[s, h]))
        c_state[h] = decay[s, h] * c_state[h] + kv[s, h]
        n_state[h] = decay[s, h] * n_state[h] + k_sum[s, h]
    for h in heads:
        c_ref[h] = c_state[h]
        n_ref[h] = n_state[h]
        m_ref[h] = jnp.broadcast_to(m_in[M_CHUNKS_PER_STEP, h], (1, LANES))

    for s, h in pairs:
        gated = _sigmoid(opre_ref[rs[s], sls[h]].astype(F32)) * hh[s, h]
        mu = jnp.mean(gated, axis=-1, keepdims=True)
        hc = gated - mu
        var = jnp.mean(hc * hc, axis=-1, keepdims=True)
        y_ref[rs[s], sls[h]] = (hc * lax.rsqrt(var + LN_EPS) * ng_ref[:, sls[h]]).astype(BF16)


def _mlstm_call(p, gif, gif_t, conv_w, conv_b, wq, wk, wv, b_i, b_f, norm_g):
    n = p.shape[0]
    t = M_CHUNK * M_CHUNKS_PER_STEP
    halo = BF16_SUBLANES
    bias = jnp.concatenate([b_i, b_f]).astype(F32)
    bias_col = jnp.zeros((1, LANES), F32).at[0, :2 * M_HEADS].set(bias)
    bias_row = jnp.broadcast_to(bias[:, None], (2 * M_HEADS, LANES))
    const = lambda *shape: pl.BlockSpec(shape, lambda c: (0,) * len(shape))
    return pl.pallas_call(
        _mlstm_kernel,
        out_shape=jax.ShapeDtypeStruct((n, D_MODEL), BF16),
        grid=(n // t,),
        in_specs=[
            pl.BlockSpec((t, D_MODEL), lambda c: (c, COL_XM)),
            pl.BlockSpec((halo, D_MODEL),
                         lambda c: (jnp.maximum(c * (t // halo) - 1, 0), COL_XM)),
            pl.BlockSpec((t, D_MODEL), lambda c: (c, COL_OPRE)),
            pl.BlockSpec((t, LANES), lambda c: (c, 0)),
            pl.BlockSpec((2 * M_HEADS, t), lambda c: (0, c)),
            const(CONV_W, D_MODEL),
            const(1, D_MODEL),
            const(M_HEADS, M_HEAD_DIM, M_HEAD_DIM),
            const(M_HEADS, M_HEAD_DIM, M_HEAD_DIM),
            const(M_HEADS, M_HEAD_DIM, M_HEAD_DIM),
            const(1, LANES),
            const(2 * M_HEADS, LANES),
            const(1, D_MODEL),
        ],
        out_specs=pl.BlockSpec((t, D_MODEL), lambda c: (c, 0)),
        scratch_shapes=[
            pltpu.VMEM((t + halo, D_MODEL), F32),
            pltpu.VMEM((M_HEADS, M_HEAD_DIM, M_HEAD_DIM), F32),
            pltpu.VMEM((M_HEADS, 1, M_HEAD_DIM), F32),
            pltpu.VMEM((M_HEADS, 1, LANES), F32),
        ],
        compiler_params=_params(("arbitrary",)),
        name="mlstm",
    )(p, p, p, gif, gif_t, conv_w, conv_b.reshape(1, -1),
      wq.astype(BF16), (wk * M_HEAD_DIM ** -0.5).astype(BF16), wv.astype(BF16),
      bias_col, bias_row,
      norm_g.reshape(1, -1))


def _attn_kernel(qt_ref, k_ref, vt_ref, lq1_ref, lk1_ref, lq2_ref, lk2_ref, g_ref,
                 o_ref, qz_ref, s0_ref, s1_ref, mt0_ref, mt1_ref, m_ref, acc_ref,
                 *, lam_init):
    i = pl.program_id(1)
    tq, tk = ATTN_TQ, ATTN_TK
    s_refs = (s0_ref, s1_ref)
    mt_refs = (mt0_ref, mt1_ref)

    for r in range(tq // tk):
        qt = qt_ref[r]
        sub = lax.broadcasted_iota(jnp.int32, qt.shape, 0)
        zero = jnp.zeros_like(qt)
        qz_ref[:, r * tk:(r + 1) * tk] = jnp.where(sub < A_HEAD_DIM, qt, zero)
        qz_ref[:, tq + r * tk:tq + (r + 1) * tk] = jnp.where(sub >= A_HEAD_DIM, qt, zero)
    m_ref[...] = jnp.full_like(m_ref, NEG)
    acc_ref[...] = jnp.zeros_like(acc_ref)

    n_blocks = 2 * tq // ATTN_LANE_BLOCK
    lane_blocks = [slice(b * ATTN_LANE_BLOCK, (b + 1) * ATTN_LANE_BLOCK) for b in range(n_blocks)]

    n_diag = tq // tk
    LEAD_TILE = "lead"
    NO_TILE = "no tile"

    def block_state(kind, b):
        if kind is None:
            return "open"
        if kind == LEAD_TILE:
            return "lead"
        q_lo = (b * ATTN_LANE_BLOCK) % tq
        if q_lo + ATTN_LANE_BLOCK <= kind * tk:
            return "skip"
        if q_lo >= (kind + 1) * tk - 1:
            return "open"
        return "causal"

    def scores_block(j, buf, kind, b):
        state = block_state(kind, b)
        if state == "skip":
            return
        lanes = lane_blocks[b]
        start = pl.multiple_of(j * tk, tk)
        kt = k_ref[pl.ds(start, tk), :]
        s = jnp.dot(kt, qz_ref[:, lanes], preferred_element_type=F32)
        if state == "lead":
            kpos = j * tk + lax.broadcasted_iota(jnp.int32, s.shape, 0)
            col = b * ATTN_LANE_BLOCK + lax.broadcasted_iota(jnp.int32, s.shape, 1)
            qpos = i * tq + jnp.where(col >= tq, col - tq, col)
            s = jnp.where(kpos <= qpos, jnp.where(kpos >= N_PAD, s, NEG), NEG)
        elif state == "causal":
            k_rel = kind * tk + lax.broadcasted_iota(jnp.int32, s.shape, 0)
            q_rel = ((b * ATTN_LANE_BLOCK) % tq
                     + lax.broadcasted_iota(jnp.int32, s.shape, 1))
            s = jnp.where(k_rel <= q_rel, s, NEG)
        s_refs[buf][:, lanes] = s
        mt_refs[buf][:, lanes] = jnp.max(s, axis=0, keepdims=True)

    ones_rows = jnp.ones((BF16_SUBLANES, tk), BF16)

    def accumulate_block(j, buf, kind, b):
        if block_state(kind, b) == "skip":
            return
        lanes = lane_blocks[b]
        m_prev = m_ref[:, lanes]
        m_new = jnp.maximum(m_prev, mt_refs[buf][:, lanes])
        alpha = jnp.exp2(m_prev - m_new)
        p = jnp.exp2((s_refs[buf][:, lanes] - m_new).astype(BF16))
        m_ref[:, lanes] = m_new
        vt = jnp.concatenate([vt_ref[j], ones_rows], axis=0)
        pv = jnp.dot(vt, p, preferred_element_type=F32)
        acc_ref[:, lanes] = alpha * acc_ref[:, lanes] + pv

    def stage(j, kind, kind_next):
        for buf in (0, 1):
            @pl.when(j % 2 == buf)
            def _():
                for b in range(n_blocks):
                    accumulate_block(j, buf, kind, b)
                    if kind_next != NO_TILE:
                        scores_block(j + 1, 1 - buf, kind_next, b)

    n_k = (i + 1) * n_diag
    first_diag = n_k - n_diag
    for b in range(n_blocks):
        scores_block(0, 0, LEAD_TILE, b)

    def body(j, carry):
        stage(j, None, None)
        return carry

    lax.fori_loop(0, first_diag - 1, body, 0)

    @pl.when(first_diag > 0)
    def _():
        stage(first_diag - 1, None, 0)

    for d in range(1, n_diag):
        stage(first_diag - 1 + d, d - 1, d)
    stage(n_k - 1, n_diag - 1, NO_TILE)

    lam = (jnp.exp(jnp.sum(lq1_ref[...] * lk1_ref[...], axis=-1, keepdims=True))
           - jnp.exp(jnp.sum(lq2_ref[...] * lk2_ref[...], axis=-1, keepdims=True))
           + lam_init)
    out = acc_ref[0:A_VDIM, :] / acc_ref[A_VDIM:A_VDIM + 1, :]
    o = out[:, 0:tq] - lam * out[:, tq:2 * tq]
    o = o * lax.rsqrt(jnp.mean(o * o, axis=0, keepdims=True) + LN_EPS)
    o = o * g_ref[...] * (1.0 - lam_init)
    o_ref[...] = o.T.astype(BF16)


def _attn_call(p, q_t, v_t, lq1, lk1, lq2, lk2, norm_g, lam_init):
    n = p.shape[0]
    tq, tk = ATTN_TQ, ATTN_TK
    hpb = D_MODEL // A_VDIM
    vec = lambda d: pl.BlockSpec((1, d), lambda h, i: (0, 0))
    return pl.pallas_call(
        functools.partial(_attn_kernel, lam_init=lam_init),
        out_shape=jax.ShapeDtypeStruct((n, D_MODEL), BF16),
        grid=(A_HEADS, n // tq),
        in_specs=[
            pl.BlockSpec((tq // tk, A_VDIM, tk), lambda h, i: (i, h, 0)),
            pl.BlockSpec((n, A_VDIM), lambda h, i: (0, COL_K * hpb + h)),
            pl.BlockSpec((n // tk, A_VDIM, tk), lambda h, i: (0, h, 0)),
            vec(A_HEAD_DIM), vec(A_HEAD_DIM), vec(A_HEAD_DIM), vec(A_HEAD_DIM),
            pl.BlockSpec((A_VDIM, 1), lambda h, i: (0, 0)),
        ],
        out_specs=pl.BlockSpec((tq, A_VDIM), lambda h, i: (i, h)),
        scratch_shapes=[
            pltpu.VMEM((A_VDIM, 2 * tq), BF16),
            pltpu.VMEM((tk, 2 * tq), F32),
            pltpu.VMEM((tk, 2 * tq), F32),
            pltpu.VMEM((1, 2 * tq), F32),
            pltpu.VMEM((1, 2 * tq), F32),
            pltpu.VMEM((1, 2 * tq), F32),
            pltpu.VMEM((A_VDIM + BF16_SUBLANES, 2 * tq), F32),
        ],
        compiler_params=_params(("parallel", "arbitrary")),
        name="diff_attn",
    )(q_t, p, v_t, lq1.reshape(1, -1), lk1.reshape(1, -1), lq2.reshape(1, -1),
      lk2.reshape(1, -1), norm_g.reshape(-1, 1))


def _merge_kernel(ym_ref, ya_ref, ga_ref, gb_ref, h_ref, wm_ref, wa_ref, wo_ref,
                  g_ref, b_ref, wr_ref, rb_ref, h1_ref, comb_ref, *, alpha):
    bm = jnp.dot(ym_ref[...], wm_ref[...], preferred_element_type=F32)
    ba = jnp.dot(ya_ref[...], wa_ref[...], preferred_element_type=F32)
    merged = (_sigmoid(ga_ref[...].astype(F32)) * bm
              + _sigmoid(gb_ref[...].astype(F32)) * ba)
    proj = jnp.dot(merged.astype(BF16), wo_ref[...], preferred_element_type=F32)
    h1 = _layer_norm(alpha * h_ref[...] + proj, g_ref[...], b_ref[...])
    h1_ref[...] = h1
    comb_ref[...] = _route(h1, wr_ref, rb_ref).T[0:N_EXPERTS, :]


def _route(h1, wr_ref, rb_ref):
    h_hi = h1.astype(BF16)
    h_lo = (h1 - h_hi.astype(F32)).astype(BF16)
    r_hi = jnp.dot(h_hi, wr_ref[...], preferred_element_type=F32)
    r_lo = jnp.dot(h_lo, wr_ref[:, 0:LANES], preferred_element_type=F32)
    logits = (r_hi[:, 0:LANES] + r_lo) + r_hi[:, LANES:2 * LANES]
    lane = lax.broadcasted_iota(jnp.int32, logits.shape, 1)
    is_expert = lane < N_EXPERTS
    logits = jnp.where(is_expert, logits, NEG)
    ex = jnp.exp(logits - jnp.max(logits, axis=-1, keepdims=True))
    probs = ex / jnp.sum(ex, axis=-1, keepdims=True)
    sel = jnp.where(is_expert, probs + rb_ref[...], NEG)
    group = lane // EXPERTS_PER_GROUP
    best = jnp.max(jnp.where(group == 0, sel, NEG), axis=-1, keepdims=True)
    top_group = jnp.zeros_like(best, dtype=jnp.int32)
    for gidx in range(1, N_GROUPS):
        score = jnp.max(jnp.where(group == gidx, sel, NEG), axis=-1, keepdims=True)
        top_group = jnp.where(score > best, gidx, top_group)
        best = jnp.maximum(best, score)
    cand = jnp.where(group == top_group, sel, NEG)
    v1 = jnp.max(cand, axis=-1, keepdims=True)
    i1 = jnp.min(jnp.where(cand == v1, lane, LANES), axis=-1, keepdims=True)
    cand2 = jnp.where(lane == i1, NEG, cand)
    v2 = jnp.max(cand2, axis=-1, keepdims=True)
    i2 = jnp.min(jnp.where(cand2 == v2, lane, LANES), axis=-1, keepdims=True)
    p1 = jnp.sum(jnp.where(lane == i1, probs, 0.0), axis=-1, keepdims=True)
    p2 = jnp.sum(jnp.where(lane == i2, probs, 0.0), axis=-1, keepdims=True)
    tot = p1 + p2
    return jnp.where(lane == i1, p1 / tot, 0.0) + jnp.where(lane == i2, p2 / tot, 0.0)


def _merge_call(y_m, y_a, p, h, w_m, w_a, w_o, ln_g, ln_b, w_router, router_bias, alpha):
    n = h.shape[0]
    tm = MERGE_TILE
    full = pl.BlockSpec((D_MODEL, D_MODEL), lambda i: (0, 0))
    row = pl.BlockSpec((1, D_MODEL), lambda i: (0, 0))
    tile = lambda col: pl.BlockSpec((tm, D_MODEL), lambda i: (i, col))
    wr32 = jnp.zeros((D_MODEL, LANES), F32).at[:, :N_EXPERTS].set(w_router)
    wr_hi = wr32.astype(BF16)
    wr = jnp.concatenate([wr_hi, (wr32 - wr_hi.astype(F32)).astype(BF16)], axis=1)
    rb = jnp.zeros((1, LANES), F32).at[0, :N_EXPERTS].set(router_bias)
    return pl.pallas_call(
        functools.partial(_merge_kernel, alpha=alpha),
        out_shape=(jax.ShapeDtypeStruct((n, D_MODEL), F32),
                   jax.ShapeDtypeStruct((N_EXPERTS, n), F32)),
        grid=(n // tm,),
        in_specs=[tile(0), tile(0), tile(COL_GA), tile(COL_GB), tile(0),
                  full, full, full, row, row,
                  pl.BlockSpec((D_MODEL, 2 * LANES), lambda i: (0, 0)),
                  pl.BlockSpec((1, LANES), lambda i: (0, 0))],
        out_specs=(tile(0), pl.BlockSpec((N_EXPERTS, tm), lambda i: (0, i))),
        compiler_params=_params(("parallel",)),
        name="merge_router",
    )(y_m, y_a, p, p, h, w_m, w_a, w_o, ln_g.reshape(1, -1), ln_b.reshape(1, -1), wr, rb)


def _moe_kernel(h_ref, combt_ref, wg_ref, wu_ref, wd_ref, g_ref, b_ref, o_ref,
                xb_ref, acc_ref, rankr_ref, *, alpha):
    e = pl.program_id(1)
    tm = MOE_TILE
    ns = MOE_SLOTS

    @pl.when(e == 0)
    def _():
        xb_ref[...] = h_ref[...].astype(BF16)
        acc_ref[...] = jnp.zeros_like(acc_ref)
        r_idx = lax.broadcasted_iota(jnp.int32, (tm, tm), 0)
        c_idx = lax.broadcasted_iota(jnp.int32, (tm, tm), 1)
        after = jnp.where(r_idx < c_idx, 1.0, 0.0).astype(BF16)
        rankr_ref[...] = jnp.dot(jnp.where(combt_ref[...] > 0.0, 1.0, 0.0).astype(BF16), after,
                                 preferred_element_type=F32)

    w_row = combt_ref[pl.ds(e, 1), :]
    routed_r = w_row > 0.0
    rank_r = rankr_ref[pl.ds(e, 1), :]
    count = jnp.sum(jnp.where(routed_r, 1.0, 0.0)).astype(jnp.int32)
    n_blocks = (count + ns - 1) // ns
    tn_dims = (((0,), (0,)), ((), ()))

    def block(b, carry):
        base = (b * ns).astype(F32)
        slot_r = lax.broadcasted_iota(jnp.int32, (ns, tm), 0).astype(F32) + base
        hit = routed_r & (rank_r == slot_r)
        sel = jnp.where(hit, 1.0, 0.0).astype(BF16)
        w_slot = jnp.sum(jnp.where(hit, w_row, 0.0), axis=-1, keepdims=True)
        xs = jnp.dot(sel, xb_ref[...], preferred_element_type=F32).astype(BF16)
        gate = jnp.dot(xs, wg_ref[0], preferred_element_type=F32)
        up = jnp.dot(xs, wu_ref[0], preferred_element_type=F32)
        mid = (gate * _sigmoid(gate) * up).astype(BF16)
        out = jnp.dot(mid, wd_ref[0], preferred_element_type=F32) * w_slot
        acc_ref[...] += lax.dot_general(sel, out.astype(BF16), tn_dims,
                                        preferred_element_type=F32)
        return carry

    lax.fori_loop(0, n_blocks, block, 0)

    @pl.when(e == N_EXPERTS - 1)
    def _():
        o_ref[...] = _layer_norm(alpha * h_ref[...] + acc_ref[...], g_ref[...], b_ref[...])


def _moe_call(h1, comb_t, w_gate, w_up, w_down, ln_g, ln_b, alpha):
    n = h1.shape[0]
    tm = MOE_TILE
    wspec = pl.BlockSpec((1, D_MODEL, D_MODEL), lambda i, e: (e, 0, 0))
    row = pl.BlockSpec((1, D_MODEL), lambda i, e: (0, 0))
    return pl.pallas_call(
        functools.partial(_moe_kernel, alpha=alpha),
        out_shape=jax.ShapeDtypeStruct((n, D_MODEL), F32),
        grid=(n // tm, N_EXPERTS),
        in_specs=[pl.BlockSpec((tm, D_MODEL), lambda i, e: (i, 0),
                               pipeline_mode=pl.Buffered(1)),
                  pl.BlockSpec((N_EXPERTS, tm), lambda i, e: (0, i)),
                  wspec, wspec, wspec, row, row],
        out_specs=pl.BlockSpec((tm, D_MODEL), lambda i, e: (i, 0)),
        scratch_shapes=[pltpu.VMEM((tm, D_MODEL), BF16),
                        pltpu.VMEM((tm, D_MODEL), F32),
                        pltpu.VMEM((N_EXPERTS, tm), F32)],
        compiler_params=_params(("parallel", "arbitrary")),
        name="moe",
    )(h1, comb_t, w_gate, w_up, w_down, ln_g.reshape(1, -1), ln_b.reshape(1, -1))


def kernel(x, meta_tokens, ln_in_g, ln_in_b, w_in, conv_w, conv_b, wq_m, wk_m, wv_m, b_i, b_f, m_norm_g, lam_q1, lam_k1, lam_q2, lam_k2, a_norm_g, w_branch_m, w_branch_a, w_out, ln1_g, ln1_b, ln2_g, ln2_b, w_router, router_bias, w_gate_e, w_up_e, w_down_e):
    bsz, seq, dm = x.shape
    assert bsz == 1 and dm == D_MODEL
    depth = w_in.shape[0]
    alpha = float((2 * depth) ** 0.25)
    length = LEAD + seq
    n = -(-length // ROW_ALIGN) * ROW_ALIGN

    assert seq % LEAD == 0
    lead = jnp.concatenate([jnp.zeros((N_PAD, dm), F32), meta_tokens.astype(F32)], axis=0)
    h = _ln_call(x[0].astype(F32), lead, ln_in_g, ln_in_b, n)

    o_if = 2 * D_MODEL
    o_rest = o_if + 2 * M_HEADS
    for l in range(depth):
        w = w_in[l]
        q_scale = A_HEAD_DIM ** -0.5 * math.log2(math.e)
        blk = lambda b: w[:, o_rest + b * D_MODEL:o_rest + (b + 1) * D_MODEL]
        w_q, w_k, w_v, w_ga, w_gb = (blk(b) for b in range(5))
        w_main = jnp.concatenate([w[:, :o_if], w_k, w_ga, w_gb, w_q * q_scale, w_v],
                                 axis=1).astype(BF16)
        w_if = jnp.zeros((dm, LANES), F32).at[:, :2 * M_HEADS].set(w[:, o_if:o_rest]).astype(BF16)
        p, gif, q_t, v_t = _inproj_call(h, w_main, w_if)
        gif_t = gif[:, :2 * M_HEADS].T
        y_m = _mlstm_call(p, gif, gif_t, conv_w[l], conv_b[l], wq_m[l], wk_m[l], wv_m[l],
                          b_i[l], b_f[l], m_norm_g[l])
        lam_init = 0.8 - 0.6 * math.exp(-0.3 * l)
        y_a = _attn_call(p, q_t, v_t, lam_q1[l], lam_k1[l], lam_q2[l], lam_k2[l], a_norm_g[l],
                         lam_init)
        h1, comb = _merge_call(y_m, y_a, p, h, w_branch_m[l].astype(BF16),
                               w_branch_a[l].astype(BF16), w_out[l].astype(BF16),
                               ln1_g[l], ln1_b[l], w_router, router_bias, alpha)
        h = _moe_call(h1, comb, w_gate_e[l].astype(BF16), w_up_e[l].astype(BF16),
                      w_down_e[l].astype(BF16), ln2_g[l], ln2_b[l], alpha)
    return h[LEAD:length][None].astype(x.dtype)
```
